```python
import jax, jax.numpy as jnp
from jax import lax
import numpy as np

D_MODEL = 4096
BATCH = 2
SEQ = 8192
DEPTH = 2

N_MEM = 256
EPS = 1e-6
ROPE_THETA = 500000.0
N_BRANCH = 4
BRANCH_WIDTH = D_MODEL // 4
MIX_WIDTH = N_BRANCH * BRANCH_WIDTH
A_HEAD_DIM = 128
A_HEADS = BRANCH_WIDTH // A_HEAD_DIM
A_ROPE_DIM = A_HEAD_DIM // 4
IDX_HEADS = 16
IDX_DIM = 64
IDX_ROPE_DIM = IDX_DIM // 4
TOPK_MAX = 256
Q_BLOCK = 128
CONV_WIDTH = 31
SG_CHUNK = 128
SG_GROUP_DIM = 128
SG_GROUPS = BRANCH_WIDTH // SG_GROUP_DIM
SC_WIDTH = 3
X_HEADS = 4
X_HEAD_DIM = 256
X_WIDTH = X_HEADS * X_HEAD_DIM
D_FF = 4 * D_MODEL
IN_SIZES = (BRANCH_WIDTH, BRANCH_WIDTH, BRANCH_WIDTH,
            IDX_HEADS * IDX_DIM, IDX_DIM, IDX_HEADS,
            2 * BRANCH_WIDTH,
            2 * BRANCH_WIDTH,
            3 * BRANCH_WIDTH)
N_IN = sum(IN_SIZES)

kernel_name = "hybrid_gated_dsa_conformer_gmlp_shortconv"


def rms_norm(x, g):
    xf = x.astype(jnp.float32)
    y = xf * lax.rsqrt(jnp.mean(xf * xf, axis=-1, keepdims=True) + EPS)
    return (y * g.astype(jnp.float32)).astype(x.dtype)


def layer_norm(x, g, b):
    xf = x.astype(jnp.float32)
    mu = jnp.mean(xf, axis=-1, keepdims=True)
    xc = xf - mu
    var = jnp.mean(xc * xc, axis=-1, keepdims=True)
    y = xc * lax.rsqrt(var + EPS) * g.astype(jnp.float32) + b.astype(jnp.float32)
    return y.astype(x.dtype)


def rope_tables(positions, rot_dim):
    inv_freq = ROPE_THETA ** (-jnp.arange(0, rot_dim, 2, dtype=jnp.float32) / rot_dim)
    ang = positions.astype(jnp.float32)[..., None] * inv_freq
    return jnp.cos(ang)[:, :, None, :], jnp.sin(ang)[:, :, None, :]


def apply_partial_rope(x, cos, sin):
    half = cos.shape[-1]
    rd = 2 * half
    xf = x.astype(jnp.float32)
    x1, x2, rest = xf[..., :half], xf[..., half:rd], xf[..., rd:]
    out = jnp.concatenate([x1 * cos - x2 * sin, x2 * cos + x1 * sin, rest], axis=-1)
    return out.astype(x.dtype)


def causal_depthwise_conv(x, w):
    width = w.shape[0]
    return lax.conv_general_dilated(
        x, w[:, None, :].astype(x.dtype), window_strides=(1,),
        padding=((width - 1, 0),), dimension_numbers=('NWC', 'WIO', 'NWC'),
        feature_group_count=x.shape[-1])


def sparse_indexed_attention(q, k, v, q_idx, k_idx, w_idx):
    b_, s_len, n_h, d_h = q.shape
    top_k = min(TOPK_MAX, s_len // 4)
    n_blocks = s_len // Q_BLOCK
    key_pos = jnp.arange(s_len)
    idx_scale = (IDX_DIM * IDX_HEADS) ** -0.5
    att_scale = d_h ** -0.5

    def block(i):
        start = i * Q_BLOCK
        qb = lax.dynamic_slice_in_dim(q, start, Q_BLOCK, axis=1)
        qib = lax.dynamic_slice_in_dim(q_idx, start, Q_BLOCK, axis=1)
        wib = lax.dynamic_slice_in_dim(w_idx, start, Q_BLOCK, axis=1)
        q_pos = start + jnp.arange(Q_BLOCK)
        rel = jax.nn.relu(jnp.einsum('bqhd,bsd->bqhs', qib, k_idx).astype(jnp.float32))
        score = jnp.einsum('bqhs,bqh->bqs', rel, wib.astype(jnp.float32)) * idx_scale
        causal = key_pos[None, :] <= q_pos[:, None]
        score = jnp.where(causal[None], score, -jnp.inf)
        _, sel = lax.top_k(score, top_k)
        valid = sel <= q_pos[None, :, None]
        k_sel = jax.vmap(lambda kb, ib: kb[ib])(k, sel)
        v_sel = jax.vmap(lambda vb, ib: vb[ib])(v, sel)
        logits = jnp.einsum('bqhd,bqkhd->bhqk', qb, k_sel).astype(jnp.float32) * att_scale
        logits = jnp.where(valid[:, None], logits, -jnp.inf)
        p = jax.nn.softmax(logits, axis=-1).astype(v.dtype)
        return jnp.einsum('bhqk,bqkhd->bqhd', p, v_sel)

    out = lax.map(block, jnp.arange(n_blocks))
    return out.transpose(1, 0, 2, 3, 4).reshape(b_, s_len, n_h * d_h)


def conformer_conv(z, w_dw, b_dw, ln_g, ln_b):
    a, gate = jnp.split(z, 2, axis=-1)
    h = a * jax.nn.sigmoid(gate)
    h = causal_depthwise_conv(h, w_dw) + b_dw
    return jax.nn.silu(layer_norm(h, ln_g, ln_b))


def chunked_spatial_gating(z, w_s, b_s, ln_g, ln_b):
    b_, s_len, _ = z.shape
    u, v = jnp.split(jax.nn.gelu(z), 2, axis=-1)
    v = layer_norm(v, ln_g, ln_b).reshape(b_, s_len // SG_CHUNK, SG_CHUNK, SG_GROUPS, SG_GROUP_DIM)
    mask = jnp.tril(jnp.ones((SG_CHUNK, SG_CHUNK), dtype=bool))
    w = jnp.where(mask[None], w_s, 0).astype(v.dtype)
    mixed = jnp.einsum('gts,bcsgd->bctgd', w, v) + b_s.T[:, :, None]
    return u * mixed.reshape(b_, s_len, BRANCH_WIDTH)


def short_gated_conv(z, w_conv):
    bg, cg, h = jnp.split(z, 3, axis=-1)
    return bg * causal_depthwise_conv(cg * h, w_conv)


def hybrid_mixer(xn, cos_a, sin_a, cos_i, sin_i, w_in, conv_dw_w, conv_dw_b, conv_ln_g, conv_ln_b,
                 sg_ln_g, sg_ln_b, sg_w, sg_b, sc_w, w_gate, b_gate, w_br, w_out):
    b_, s_len, _ = xn.shape
    split_points = [int(p) for p in np.cumsum(IN_SIZES)[:-1]]
    z = xn @ w_in
    zq, zk, zv, zqi, zki, zwi, z_b, z_c, z_d = jnp.split(z, split_points, axis=-1)
    q = apply_partial_rope(zq.reshape(b_, s_len, A_HEADS, A_HEAD_DIM), cos_a, sin_a)
    k = apply_partial_rope(zk.reshape(b_, s_len, A_HEADS, A_HEAD_DIM), cos_a, sin_a)
    v = zv.reshape(b_, s_len, A_HEADS, A_HEAD_DIM)
    qi = apply_partial_rope(zqi.reshape(b_, s_len, IDX_HEADS, IDX_DIM), cos_i, sin_i)
    ki = apply_partial_rope(zki[:, :, None, :], cos_i, sin_i)[:, :, 0]
    o_a = sparse_indexed_attention(q, k, v, qi, ki, zwi)
    o_b = conformer_conv(z_b, conv_dw_w, conv_dw_b, conv_ln_g, conv_ln_b)
    o_c = chunked_spatial_gating(z_c, sg_w, sg_b, sg_ln_g, sg_ln_b)
    o_d = short_gated_conv(z_d, sc_w)
    y = jnp.zeros_like(xn)
    for n, o in enumerate((o_a, o_b, o_c, o_d)):
        g = jax.nn.sigmoid(xn @ w_gate[n] + b_gate[n])
        y = y + g * (o @ w_br[n * BRANCH_WIDTH:(n + 1) * BRANCH_WIDTH])
    return y @ w_out


def memory_cross_attention(hn, memn, wq, wk, wv, wo):
    b_, s_len, _ = hn.shape
    m = memn.shape[1]
    q = (hn @ wq).reshape(b_, s_len, X_HEADS, X_HEAD_DIM)
    k = (memn @ wk).reshape(b_, m, X_HEADS, X_HEAD_DIM)
    v = (memn @ wv).reshape(b_, m, X_HEADS, X_HEAD_DIM)
    logits = jnp.einsum('bshd,bmhd->bhsm', q, k).astype(jnp.float32) * (X_HEAD_DIM ** -0.5)
    p = jax.nn.softmax(logits, axis=-1).astype(v.dtype)
    o = jnp.einsum('bhsm,bmhd->bshd', p, v).reshape(b_, s_len, X_WIDTH)
    return o @ wo


def squared_relu_mlp(h, w1, w2):
    return jnp.square(jax.nn.relu(h @ w1)) @ w2


def setup_inputs(seed: int = 0) -> dict:
    key = jax.random.key(seed)
    keys = iter(jax.random.split(key, 40))

    def normal(shape, scale):
        return jax.random.normal(next(keys), shape, dtype=jnp.float32) * scale

    def gain(shape):
        return 1.0 + normal(shape, 0.02)

    x = normal((BATCH, SEQ, D_MODEL), 1.0)
    mem = normal((BATCH, N_MEM, D_MODEL), 1.0)
    start = jax.random.randint(next(keys), (BATCH, 1), 0, 1024, dtype=jnp.int32)
    positions = (start + jnp.arange(SEQ, dtype=jnp.int32)[None, :]).astype(jnp.int32)
    L = DEPTH
    return {
        'x': x, 'mem': mem, 'positions': positions,
        'norm_mix': gain((L, D_MODEL)),
        'w_in': normal((L, D_MODEL, N_IN), D_MODEL ** -0.5),
        'conv_dw_w': normal((L, CONV_WIDTH, BRANCH_WIDTH), CONV_WIDTH ** -0.5),
        'conv_dw_b': normal((L, BRANCH_WIDTH), 0.02),
        'conv_ln_g': gain((L, BRANCH_WIDTH)),
        'conv_ln_b': normal((L, BRANCH_WIDTH), 0.02),
        'sg_ln_g': gain((L, BRANCH_WIDTH)),
        'sg_ln_b': normal((L, BRANCH_WIDTH), 0.02),
        'sg_w': normal((L, SG_GROUPS, SG_CHUNK, SG_CHUNK), SG_CHUNK ** -0.5),
        'sg_b': gain((L, SG_GROUPS, SG_CHUNK)),
        'sc_w': normal((L, SC_WIDTH, BRANCH_WIDTH), SC_WIDTH ** -0.5),
        'w_gate': normal((L, N_BRANCH, D_MODEL, D_MODEL), D_MODEL ** -0.5),
        'b_gate': normal((L, N_BRANCH, D_MODEL), 0.02),
        'w_br': normal((L, MIX_WIDTH, D_MODEL), BRANCH_WIDTH ** -0.5),
        'w_out': normal((L, D_MODEL, D_MODEL), D_MODEL ** -0.5),
        'norm_xattn': gain((L, D_MODEL)),
        'norm_mem': gain((L, D_MODEL)),
        'xq_w': normal((L, D_MODEL, X_WIDTH), D_MODEL ** -0.5),
        'xk_w': normal((L, D_MODEL, X_WIDTH), D_MODEL ** -0.5),
        'xv_w': normal((L, D_MODEL, X_WIDTH), D_MODEL ** -0.5),
        'xo_w': normal((L, X_WIDTH, D_MODEL), X_WIDTH ** -0.5),
        'norm_mlp': gain((L, D_MODEL)),
        'mlp_w1': normal((L, D_MODEL, D_FF), D_MODEL ** -0.5),
        'mlp_w2': normal((L, D_FF, D_MODEL), D_FF ** -0.5),
        'final_norm': gain((D_MODEL,)),
    }


def reference(x, mem, positions, norm_mix, w_in, conv_dw_w, conv_dw_b, conv_ln_g, conv_ln_b,
              sg_ln_g, sg_ln_b, sg_w, sg_b, sc_w, w_gate, b_gate, w_br, w_out,
              norm_xattn, norm_mem, xq_w, xk_w, xv_w, xo_w, norm_mlp, mlp_w1, mlp_w2, final_norm):
    cos_a, sin_a = rope_tables(positions, A_ROPE_DIM)
    cos_i, sin_i = rope_tables(positions, IDX_ROPE_DIM)
    h = x
    for l in range(DEPTH):
        h = h + hybrid_mixer(rms_norm(h, norm_mix[l]), cos_a, sin_a, cos_i, sin_i, w_in[l],
                             conv_dw_w[l], conv_dw_b[l], conv_ln_g[l], conv_ln_b[l],
                             sg_ln_g[l], sg_ln_b[l], sg_w[l], sg_b[l], sc_w[l],
                             w_gate[l], b_gate[l], w_br[l], w_out[l])
        h = h + memory_cross_attention(rms_norm(h, norm_xattn[l]), rms_norm(mem, norm_mem[l]),
                                       xq_w[l], xk_w[l], xv_w[l], xo_w[l])
        h = h + squared_relu_mlp(rms_norm(h, norm_mlp[l]), mlp_w1[l], mlp_w2[l])
    return rms_norm(h, final_norm)
```

```python
import functools

import jax
import jax.numpy as jnp
from jax import lax
from jax.experimental import pallas as pl
from jax.experimental.pallas import tpu as pltpu

F32 = jnp.float32
BF16 = jnp.bfloat16
I32 = jnp.int32

EPS = 1e-6
ROPE_THETA = 500000.0
LANES = 128
A_HEAD_DIM = 128
A_ROPE_DIM = A_HEAD_DIM // 4
IDX_HEADS = 16
IDX_DIM = 64
IDX_WIDTH = IDX_HEADS * IDX_DIM
IDX_ROPE_DIM = IDX_DIM // 4
TOPK_MAX = 256
SG_CHUNK = 128
X_HEADS = 4
NEG = -1e30
INT_MIN = -2147483648
VMEM_LIMIT = 56 * 1024 * 1024
CONV_HALO = 32
SC_HALO = 8
SMALL_N = 2 * LANES


def _params(*sem):
    return pltpu.CompilerParams(dimension_semantics=sem, vmem_limit_bytes=VMEM_LIMIT)


def _dot(a, b):
    return jnp.dot(a, b, preferred_element_type=F32)


def _dot_nt(a, b):
    return lax.dot_general(a, b, (((1,), (1,)), ((), ())), preferred_element_type=F32)


def _rmsnorm_body(x_ref, g_ref, o_ref):
    x = x_ref[...]
    ms = jnp.mean(x * x, axis=-1, keepdims=True)
    o_ref[...] = ((x * lax.rsqrt(ms + EPS)) * g_ref[...]).astype(o_ref.dtype)


def _rmsnorm(x, g, out_dtype):
    m, d = x.shape
    tr = min(256, m)
    return pl.pallas_call(
        _rmsnorm_body,
        grid=(m // tr,),
        in_specs=[pl.BlockSpec((tr, d), lambda i: (i, 0)),
                  pl.BlockSpec((1, d), lambda i: (0, 0))],
        out_specs=pl.BlockSpec((tr, d), lambda i: (i, 0)),
        out_shape=jax.ShapeDtypeStruct((m, d), out_dtype),
        compiler_params=_params("parallel"),
    )(x, g.reshape(1, d))


def _mm_body(*refs, nk, epilogue, has_res):
    a_ref, b_ref = refs[0], refs[1]
    res_ref = refs[2] if has_res else None
    o_ref = refs[3] if has_res else refs[2]
    acc_ref = refs[-1] if nk > 1 else None

    def finish(acc):
        if epilogue == "relu2":
            r = jnp.maximum(acc, 0.0)
            acc = r * r
        if has_res:
            acc = res_ref[...] + acc
        o_ref[...] = acc.astype(o_ref.dtype)

    part = _dot(a_ref[...], b_ref[...])
    if nk == 1:
        finish(part)
        return
    k = pl.program_id(2)

    @pl.when(k == 0)
    def _():
        acc_ref[...] = part

    @pl.when(k > 0)
    def _():
        acc_ref[...] += part

    @pl.when(k == nk - 1)
    def _():
        finish(acc_ref[...])


def _matmul(a, b, out_dtype, res=None, epilogue=None, tm=1024, tn=1024, tk=4096):
    m, kdim = a.shape
    n = b.shape[1]
    tm, tn, tk = min(tm, m), min(tn, n), min(tk, kdim)
    nk = kdim // tk
    in_specs = [pl.BlockSpec((tm, tk), lambda i, j, k: (i, k)),
                pl.BlockSpec((tk, tn), lambda i, j, k: (k, j))]
    args = [a, b]
    if res is not None:
        in_specs.append(pl.BlockSpec((tm, tn), lambda i, j, k: (i, j)))
        args.append(res)
    return pl.pallas_call(
        functools.partial(_mm_body, nk=nk, epilogue=epilogue, has_res=res is not None),
        grid=(m // tm, n // tn, nk),
        in_specs=in_specs,
        out_specs=pl.BlockSpec((tm, tn), lambda i, j, k: (i, j)),
        out_shape=jax.ShapeDtypeStruct((m, n), out_dtype),
        scratch_shapes=[pltpu.VMEM((tm, tn), F32)] if nk > 1 else [],
        compiler_params=_params("parallel", "parallel", "arbitrary"),
    )(*args)


def _tables_body(pos_ref, inva_ref, invi_ref, ca_ref, sa_ref, ci_ref, si_ref):
    p = pos_ref[...]
    ang_a = p * inva_ref[...]
    ang_i = p * invi_ref[...]
    ca_ref[...] = jnp.cos(ang_a)
    sa_ref[...] = jnp.sin(ang_a)
    ci_ref[...] = jnp.cos(ang_i)
    si_ref[...] = jnp.sin(ang_i)


def _rope_tables(positions):
    b, s = positions.shape
    m = b * s
    pos = jnp.broadcast_to(positions.astype(F32).reshape(m, 1), (m, LANES))

    def inv_row(rot, period):
        inv = ROPE_THETA ** (-jnp.arange(0, rot, 2, dtype=F32) / rot)
        one = jnp.concatenate([inv, inv, jnp.zeros((period - rot,), F32)])
        return jnp.tile(one, LANES // period).reshape(1, LANES)

    tr = min(1024, m)
    spec = pl.BlockSpec((tr, LANES), lambda i: (i, 0))
    row = pl.BlockSpec((1, LANES), lambda i: (0, 0))
    sds = jax.ShapeDtypeStruct((m, LANES), F32)
    return pl.pallas_call(
        _tables_body,
        grid=(m // tr,),
        in_specs=[spec, row, row],
        out_specs=[spec] * 4,
        out_shape=[sds] * 4,
        compiler_params=_params("parallel"),
    )(pos, inv_row(A_ROPE_DIM, A_HEAD_DIM), inv_row(IDX_ROPE_DIM, IDX_DIM))


def _rope_group(x, cos, sin, first_half, half):
    rot = jnp.where(first_half, -pltpu.roll(x, LANES - half, 1), pltpu.roll(x, half, 1))
    return x * cos + sin * rot


def _first_half_mask(rows, period, half):
    lane = lax.broadcasted_iota(I32, (rows, LANES), 1)
    return (lane & (period - 1)) < half


def _win_body(a_ref, b_ref, ca_ref, sa_ref, ci_ref, si_ref, o_ref, *, n_qi):
    j = pl.program_id(1)
    acc = _dot(a_ref[...], b_ref[...])
    tm, tn = acc.shape

    def roped(cos_ref, sin_ref, period, half):
        first = _first_half_mask(tm, period, half)
        cos, sin = cos_ref[...], sin_ref[...]
        for g in range(tn // LANES):
            sl = slice(g * LANES, (g + 1) * LANES)
            o_ref[:, sl] = _rope_group(acc[:, sl], cos, sin, first, half).astype(o_ref.dtype)

    is_i = j < n_qi
    is_a = jnp.logical_and(j >= n_qi, j < n_qi + 2)

    @pl.when(is_a)
    def _():
        roped(ca_ref, sa_ref, A_HEAD_DIM, A_ROPE_DIM // 2)

    @pl.when(is_i)
    def _():
        roped(ci_ref, si_ref, IDX_DIM, IDX_ROPE_DIM // 2)

    @pl.when(jnp.logical_not(jnp.logical_or(is_a, is_i)))
    def _():
        o_ref[...] = acc.astype(o_ref.dtype)


def _in_proj(xn, w_big, tables, bw):
    m, d = xn.shape
    n = w_big.shape[1]
    tm, tn = min(1024, m), bw
    ca, sa, ci, si = tables
    tab = pl.BlockSpec((tm, LANES), lambda i, j: (i, 0))
    return pl.pallas_call(
        functools.partial(_win_body, n_qi=IDX_WIDTH // bw),
        grid=(m // tm, n // tn),
        in_specs=[pl.BlockSpec((tm, d), lambda i, j: (i, 0)),
                  pl.BlockSpec((d, tn), lambda i, j: (0, j)),
                  tab, tab, tab, tab],
        out_specs=pl.BlockSpec((tm, tn), lambda i, j: (i, j)),
        out_shape=jax.ShapeDtypeStruct((m, n), BF16),
        compiler_params=_params("parallel", "parallel"),
    )(xn, w_big, ca, sa, ci, si)


def _wsmall_body(a_ref, b_ref, ci_ref, si_ref, o_ref):
    acc = _dot(a_ref[...], b_ref[...])
    first = _first_half_mask(acc.shape[0], IDX_DIM, IDX_ROPE_DIM // 2)
    o_ref[:, :LANES] = _rope_group(acc[:, :LANES], ci_ref[...], si_ref[...], first, IDX_ROPE_DIM // 2)
    o_ref[:, LANES:] = acc[:, LANES:]


def _in_proj_small(xn, w_small, tables):
    m, d = xn.shape
    tm = min(1024, m)
    tab = pl.BlockSpec((tm, LANES), lambda i: (i, 0))
    return pl.pallas_call(
        _wsmall_body,
        grid=(m // tm,),
        in_specs=[pl.BlockSpec((tm, d), lambda i: (i, 0)),
                  pl.BlockSpec((d, SMALL_N), lambda i: (0, 0)),
                  tab, tab],
        out_specs=pl.BlockSpec((tm, SMALL_N), lambda i: (i, 0)),
        out_shape=jax.ShapeDtypeStruct((m, SMALL_N), F32),
        compiler_params=_params("parallel"),
    )(xn, w_small, tables[2], tables[3])


def _sortable(score):
    bits = lax.bitcast_convert_type(score, I32)
    return jnp.where(bits < 0, bits ^ jnp.int32(0x7FFFFFFF), bits)


def _index_body(qi_ref, w_ref, ki_ref, bias_ref, kbf_scr, qm_scr, sc_scr, jthr_scr,
                *, tq, topk, scale, s_len):
    i = pl.program_id(1)
    n_chunks = s_len // tq
    n_live = i + 1

    @pl.when(i == 0)
    def _():
        kbf_scr[...] = ki_ref[...].astype(BF16)

    lane = lax.broadcasted_iota(I32, (tq, LANES), 1)
    for p in range(IDX_HEADS // 2):
        qp = qi_ref[:, p * LANES:(p + 1) * LANES]
        zero = jnp.zeros_like(qp)
        qm_scr[2 * p] = jnp.where(lane < IDX_DIM, qp, zero)
        qm_scr[2 * p + 1] = jnp.where(lane >= IDX_DIM, qp, zero)
    w_t = w_ref[...].T

    def chunk_rows(c):
        return pl.ds(pl.multiple_of(c * tq, tq), tq)

    for h in range(IDX_HEADS):
        qm = qm_scr[h]
        wh = w_t[h:h + 1, :]

        def head_body(c, carry, qm=qm, wh=wh, h=h):
            rows = chunk_rows(c)
            rel = jnp.maximum(_dot_nt(kbf_scr[rows, :], qm), 0.0) * wh
            if h == 0:
                sc_scr[rows, :] = rel
            else:
                sc_scr[rows, :] += rel
            return carry

        lax.fori_loop(0, n_live, head_body, 0)

    q_pos = i * tq + lax.broadcasted_iota(I32, (tq, tq), 1)
    k_off = lax.broadcasted_iota(I32, (tq, tq), 0)

    def key_body(c, carry):
        rows = chunk_rows(c)
        key = _sortable(sc_scr[rows, :] * scale)
        key = jnp.where(c * tq + k_off <= q_pos, key, jnp.int32(INT_MIN))
        sc_scr[rows, :] = lax.bitcast_convert_type(key, F32)
        return carry

    lax.fori_loop(0, n_live, key_body, 0)

    def load_keys(c):
        return lax.bitcast_convert_type(sc_scr[chunk_rows(c), :], I32)

    def count(pred):
        def body(c, cnt8):
            hit = pred(load_keys(c), c * tq + k_off).astype(I32)
            return cnt8 + hit.reshape(tq // 8, 8, tq).sum(axis=0)
        cnt8 = lax.fori_loop(0, n_live, body, jnp.zeros((8, tq), I32))
        return cnt8.sum(axis=0, keepdims=True)

    cnt0 = count(lambda key, idx: key >= 0)
    prefix0 = jnp.where(cnt0 >= topk, jnp.int32(0), jnp.int32(INT_MIN))

    def bit_body(it, prefix):
        cand = prefix | jnp.left_shift(jnp.int32(1), jnp.int32(30) - it)
        cnt = count(lambda key, idx: key >= cand)
        return jnp.where(cnt >= topk, cand, prefix)

    thr = lax.fori_loop(0, 31, bit_body, prefix0)

    cnt_gt = count(lambda key, idx: key > thr)
    cnt_ge = count(lambda key, idx: key >= thr)
    need = topk - cnt_gt
    tie = jnp.logical_and(cnt_ge > topk, thr != INT_MIN)
    jthr_scr[...] = jnp.full((8, tq), s_len, I32)

    @pl.when(jnp.max(tie.astype(I32)) > 0)
    def _():
        n_bits = max(1, (s_len - 1).bit_length())

        def idx_body(it, ans):
            cand = ans | jnp.left_shift(jnp.int32(1), jnp.int32(n_bits - 1) - it)
            cnt = count(lambda key, idx: jnp.logical_and(key == thr, idx < cand))
            return jnp.where(cnt < need, cand, ans)

        ans = lax.fori_loop(0, n_bits, idx_body, jnp.zeros((1, tq), I32))
        jthr_scr[...] = jnp.broadcast_to(jnp.where(tie, ans, s_len), (8, tq))

    jthr = jthr_scr[0:1, :]

    def out_body(c, carry):
        key = load_keys(c)
        idx = c * tq + k_off
        sel = jnp.logical_or(key > thr, jnp.logical_and(key == thr, idx <= jthr))
        sel = jnp.logical_and(sel, idx <= q_pos)
        bias_t = jnp.where(sel, 0.0, NEG)
        bias_ref[0, 0, c] = bias_t.T.astype(bias_ref.dtype)
        return carry

    lax.fori_loop(0, n_live, out_body, 0)

    def fill_body(c, carry):
        bias_ref[0, 0, c] = jnp.full((tq, tq), NEG, bias_ref.dtype)
        return carry

    lax.fori_loop(n_live, n_chunks, fill_body, 0)


def _index_mask(z, kiwi, b, s_len, bw, tq):
    nq = s_len // tq
    topk = min(TOPK_MAX, s_len // 4)
    qi_col = 0
    return pl.pallas_call(
        functools.partial(_index_body, tq=tq, topk=topk, s_len=s_len,
                          scale=float((IDX_DIM * IDX_HEADS) ** -0.5)),
        grid=(b, nq),
        in_specs=[pl.BlockSpec((tq, IDX_WIDTH), lambda bi, i: (bi * nq + i, qi_col)),
                  pl.BlockSpec((tq, LANES), lambda bi, i: (bi * nq + i, 1)),
                  pl.BlockSpec((s_len, LANES), lambda bi, i: (bi, 0))],
        out_specs=pl.BlockSpec((1, 1, nq, tq, tq), lambda bi, i: (bi, i, 0, 0, 0)),
        out_shape=jax.ShapeDtypeStruct((b, nq, nq, tq, tq), BF16),
        scratch_shapes=[pltpu.VMEM((s_len, LANES), BF16),
                        pltpu.VMEM((IDX_HEADS, tq, LANES), BF16),
                        pltpu.VMEM((s_len, tq), F32),
                        pltpu.VMEM((8, tq), I32)],
        compiler_params=_params("arbitrary", "arbitrary"),
    )(z, kiwi, kiwi)


def _flash_body(q_ref, k_ref, v_ref, b_ref, o_ref, m_scr, l_scr, acc_scr, *, heads, scale, nk):
    i, j = pl.program_id(1), pl.program_id(2)
    tk = k_ref.shape[0]

    @pl.when(j == 0)
    def _():
        m_scr[...] = jnp.full(m_scr.shape, NEG, F32)
        l_scr[...] = jnp.zeros(l_scr.shape, F32)
        acc_scr[...] = jnp.zeros(acc_scr.shape, F32)

    @pl.when(j <= i)
    def _():
        bias = b_ref[0, 0, 0].astype(F32)
        for h in range(heads):
            sl = slice(h * A_HEAD_DIM, (h + 1) * A_HEAD_DIM)
            s = _dot_nt(q_ref[:, sl], k_ref[:, sl]) * scale + bias
            m_prev = m_scr[h]
            m_new = jnp.maximum(m_prev, jnp.max(s, axis=1, keepdims=True))
            alpha = jnp.exp(m_prev - m_new)
            p = jnp.exp(s - jnp.tile(m_new, (1, tk // LANES)))
            l_scr[h] = alpha * l_scr[h] + jnp.sum(p, axis=1, keepdims=True)
            acc_scr[:, sl] = alpha * acc_scr[:, sl] + _dot(p.astype(BF16), v_ref[:, sl])
            m_scr[h] = m_new

    @pl.when(j == nk - 1)
    def _():
        for h in range(heads):
            sl = slice(h * A_HEAD_DIM, (h + 1) * A_HEAD_DIM)
            o_ref[:, sl] = (acc_scr[:, sl] / l_scr[h]).astype(o_ref.dtype)


def _masked_attention(z, bias, b, s_len, bw, col, tq):
    nq = s_len // tq
    heads = bw // A_HEAD_DIM
    return pl.pallas_call(
        functools.partial(_flash_body, heads=heads, scale=float(A_HEAD_DIM ** -0.5), nk=nq),
        grid=(b, nq, nq),
        in_specs=[pl.BlockSpec((tq, bw), lambda bi, i, j: (bi * nq + i, col)),
                  pl.BlockSpec((tq, bw), lambda bi, i, j: (bi * nq + jnp.minimum(j, i), col + 1)),
                  pl.BlockSpec((tq, bw), lambda bi, i, j: (bi * nq + jnp.minimum(j, i), col + 2)),
                  pl.BlockSpec((1, 1, 1, tq, tq), lambda bi, i, j: (bi, i, jnp.minimum(j, i), 0, 0))],
        out_specs=pl.BlockSpec((tq, bw), lambda bi, i, j: (bi * nq + i, 0)),
        out_shape=jax.ShapeDtypeStruct((b * s_len, bw), BF16),
        scratch_shapes=[pltpu.VMEM((heads, tq, LANES), F32),
                        pltpu.VMEM((heads, tq, LANES), F32),
                        pltpu.VMEM((tq, bw), F32)],
        compiler_params=_params("parallel", "parallel", "arbitrary"),
    )(z, z, z, bias)


def _conformer_body(a_ref, g_ref, ah_ref, gh_ref, w_ref, bdw_ref, lng_ref, lnb_ref, o_ref,
                    hbuf, ybuf, *, width):
    i = pl.program_id(1)
    ts, bw = a_ref.shape
    ncg = bw // LANES
    h_main = a_ref[...].astype(F32) * jax.nn.sigmoid(g_ref[...].astype(F32))
    h_halo = ah_ref[...].astype(F32) * jax.nn.sigmoid(gh_ref[...].astype(F32))
    h_halo = jnp.where(i == 0, 0.0, h_halo)
    for c in range(ncg):
        sl = slice(c * LANES, (c + 1) * LANES)
        hbuf[c, 0:CONV_HALO, :] = h_halo[:, sl]
        hbuf[c, CONV_HALO:, :] = h_main[:, sl]

    def col_body(c, carry):
        wc = w_ref[c]
        acc = jnp.zeros((ts, LANES), F32)
        for k in range(width):
            off = CONV_HALO - (width - 1) + k
            acc = acc + hbuf[c, pl.ds(off, ts), :] * wc[k:k + 1, :]
        ybuf[c] = acc + bdw_ref[c]
        return carry

    lax.fori_loop(0, ncg, col_body, 0)

    tot = ybuf[0]
    for c in range(1, ncg):
        tot = tot + ybuf[c]
    mu = jnp.sum(tot, axis=1, keepdims=True) / bw
    sq = jnp.zeros((ts, LANES), F32)
    for c in range(ncg):
        d = ybuf[c] - mu
        sq = sq + d * d
    inv = lax.rsqrt(jnp.sum(sq, axis=1, keepdims=True) / bw + EPS)
    for c in range(ncg):
        y = (ybuf[c] - mu) * inv * lng_ref[c] + lnb_ref[c]
        o_ref[:, c * LANES:(c + 1) * LANES] = (y * jax.nn.sigmoid(y)).astype(o_ref.dtype)


def _lane_groups(v):
    return v.reshape(-1, 1, LANES)


def _conformer(z, b, s_len, bw, col, w_dw, b_dw, ln_g, ln_b, ts=256):
    width = w_dw.shape[0]
    ncg = bw // LANES
    ns = s_len // ts
    hb = ts // CONV_HALO
    w_cg = w_dw.reshape(width, ncg, LANES).transpose(1, 0, 2)
    main = lambda cb: pl.BlockSpec((ts, bw), lambda bi, i: (bi * ns + i, cb))
    halo = lambda cb: pl.BlockSpec(
        (CONV_HALO, bw), lambda bi, i: ((bi * ns + i) * hb - jnp.minimum(i, 1), cb))
    small = pl.BlockSpec((ncg, 1, LANES), lambda bi, i: (0, 0, 0))
    return pl.pallas_call(
        functools.partial(_conformer_body, width=width),
        grid=(b, ns),
        in_specs=[main(col), main(col + 1), halo(col), halo(col + 1),
                  pl.BlockSpec((ncg, width, LANES), lambda bi, i: (0, 0, 0)),
                  small, small, small],
        out_specs=pl.BlockSpec((ts, bw), lambda bi, i: (bi * ns + i, 0)),
        out_shape=jax.ShapeDtypeStruct((b * s_len, bw), BF16),
        scratch_shapes=[pltpu.VMEM((ncg, ts + CONV_HALO, LANES), F32),
                        pltpu.VMEM((ncg, ts, LANES), F32)],
        compiler_params=_params("parallel", "parallel"),
    )(z, z, z, z, w_cg, _lane_groups(b_dw), _lane_groups(ln_g), _lane_groups(ln_b))


def _gmlp_body(u_ref, v_ref, lng_ref, lnb_ref, w_ref, bias_ref, o_ref):
    ts, bw = u_ref.shape
    groups = bw // LANES
    u = jax.nn.gelu(u_ref[...].astype(F32))
    v = jax.nn.gelu(v_ref[...].astype(F32))
    mu = jnp.mean(v, axis=-1, keepdims=True)
    vc = v - mu
    var = jnp.mean(vc * vc, axis=-1, keepdims=True)
    vn = (vc * lax.rsqrt(var + EPS) * lng_ref[...] + lnb_ref[...]).astype(BF16)
    row = lax.broadcasted_iota(I32, (SG_CHUNK, SG_CHUNK), 0)
    col = lax.broadcasted_iota(I32, (SG_CHUNK, SG_CHUNK), 1)
    for g in range(groups):
        cs = slice(g * LANES, (g + 1) * LANES)
        wg = jnp.where(row >= col, w_ref[g], 0.0).astype(BF16)
        for c in range(ts // SG_CHUNK):
            rs = slice(c * SG_CHUNK, (c + 1) * SG_CHUNK)
            mixed = _dot(wg, vn[rs, cs]) + bias_ref[:, cs]
            o_ref[rs, cs] = (u[rs, cs] * mixed).astype(o_ref.dtype)


def _gmlp(z, m, bw, col, ln_g, ln_b, sg_w, sg_b, ts=256):
    groups = bw // LANES
    bias = jnp.broadcast_to(sg_b.T[:, :, None], (SG_CHUNK, groups, LANES)).reshape(SG_CHUNK, bw)
    row = pl.BlockSpec((1, bw), lambda i: (0, 0))
    return pl.pallas_call(
        _gmlp_body,
        grid=(m // ts,),
        in_specs=[pl.BlockSpec((ts, bw), lambda i: (i, col)),
                  pl.BlockSpec((ts, bw), lambda i: (i, col + 1)),
                  row, row,
                  pl.BlockSpec((groups, SG_CHUNK, SG_CHUNK), lambda i: (0, 0, 0)),
                  pl.BlockSpec((SG_CHUNK, bw), lambda i: (0, 0))],
        out_specs=pl.BlockSpec((ts, bw), lambda i: (i, 0)),
        out_shape=jax.ShapeDtypeStruct((m, bw), BF16),
        compiler_params=_params("parallel"),
    )(z, z, ln_g.reshape(1, bw), ln_b.reshape(1, bw), sg_w, bias)


def _shortconv_body(bg_ref, cg_ref, h_ref, cgh_ref, hh_ref, w_ref, o_ref, xbuf, *, width):
    i = pl.program_id(1)
    ts = bg_ref.shape[0]
    x_halo = cgh_ref[...].astype(F32) * hh_ref[...].astype(F32)
    xbuf[0:SC_HALO, :] = jnp.where(i == 0, 0.0, x_halo)
    xbuf[SC_HALO:, :] = cg_ref[...].astype(F32) * h_ref[...].astype(F32)
    acc = jnp.zeros(o_ref.shape, F32)
    for k in range(width):
        off = SC_HALO - (width - 1) + k
        acc = acc + xbuf[pl.ds(off, ts), :] * w_ref[k:k + 1, :]
    o_ref[...] = (bg_ref[...].astype(F32) * acc).astype(o_ref.dtype)


def _shortconv(z, b, s_len, bw, col, w_conv, ts=256):
    width = w_conv.shape[0]
    ns = s_len // ts
    hb = ts // SC_HALO
    main = lambda cb: pl.BlockSpec((ts, bw), lambda bi, i: (bi * ns + i, cb))
    halo = lambda cb: pl.BlockSpec(
        (SC_HALO, bw), lambda bi, i: ((bi * ns + i) * hb - jnp.minimum(i, 1), cb))
    return pl.pallas_call(
        functools.partial(_shortconv_body, width=width),
        grid=(b, ns),
        in_specs=[main(col), main(col + 1), main(col + 2), halo(col + 1), halo(col + 2),
                  pl.BlockSpec((width, bw), lambda bi, i: (0, 0))],
        out_specs=pl.BlockSpec((ts, bw), lambda bi, i: (bi * ns + i, 0)),
        out_shape=jax.ShapeDtypeStruct((b * s_len, bw), BF16),
        scratch_shapes=[pltpu.VMEM((ts + SC_HALO, bw), F32)],
        compiler_params=_params("parallel", "parallel"),
    )(z, z, z, z, z, w_conv)


def _combine_body(xn_ref, wg_ref, bg_ref, o_ref, wbr_ref, y_ref, acc_ref, *, n_branch):
    n = pl.program_id(2)
    gate = jax.nn.sigmoid(_dot(xn_ref[...], wg_ref[0]) + bg_ref[0])
    term = gate * _dot(o_ref[0], wbr_ref[0])

    @pl.when(n == 0)
    def _():
        acc_ref[...] = term

    @pl.when(n > 0)
    def _():
        acc_ref[...] += term

    @pl.when(n == n_branch - 1)
    def _():
        y_ref[...] = acc_ref[...].astype(y_ref.dtype)


def _combine(xn, w_gate, b_gate, o4, w_br4, tm=1024, tn=512):
    m, d = xn.shape
    n_branch, bw = o4.shape[0], o4.shape[2]
    tm, tn = min(tm, m), min(tn, d)
    return pl.pallas_call(
        functools.partial(_combine_body, n_branch=n_branch),
        grid=(m // tm, d // tn, n_branch),
        in_specs=[pl.BlockSpec((tm, d), lambda i, j, n: (i, 0)),
                  pl.BlockSpec((1, d, tn), lambda i, j, n: (n, 0, j)),
                  pl.BlockSpec((1, 1, tn), lambda i, j, n: (n, 0, j)),
                  pl.BlockSpec((1, tm, bw), lambda i, j, n: (n, i, 0)),
                  pl.BlockSpec((1, bw, tn), lambda i, j, n: (n, 0, j))],
        out_specs=pl.BlockSpec((tm, tn), lambda i, j, n: (i, j)),
        out_shape=jax.ShapeDtypeStruct((m, d), BF16),
        scratch_shapes=[pltpu.VMEM((tm, tn), F32)],
        compiler_params=_params("parallel", "parallel", "arbitrary"),
    )(xn, w_gate, b_gate.reshape(n_branch, 1, d), o4, w_br4)


def _xattn_body(q_ref, k_ref, v_ref, o_ref, *, heads, scale):
    hd = q_ref.shape[1] // heads
    for h in range(heads):
        sl = slice(h * hd, (h + 1) * hd)
        s = _dot_nt(q_ref[:, sl], k_ref[:, sl]) * scale
        p = jnp.exp(s - jnp.max(s, axis=1, keepdims=True))
        l = jnp.sum(p, axis=1, keepdims=True)
        o_ref[:, sl] = (_dot(p.astype(BF16), v_ref[:, sl]) / l).astype(o_ref.dtype)


def _xattn(q, k, v, b, s_len, n_mem, tq=512):
    xw = q.shape[1]
    tq = min(tq, s_len)
    nq = s_len // tq
    kv = pl.BlockSpec((n_mem, xw), lambda bi, i: (bi, 0))
    return pl.pallas_call(
        functools.partial(_xattn_body, heads=X_HEADS, scale=float((xw // X_HEADS) ** -0.5)),
        grid=(b, nq),
        in_specs=[pl.BlockSpec((tq, xw), lambda bi, i: (bi * nq + i, 0)), kv, kv],
        out_specs=pl.BlockSpec((tq, xw), lambda bi, i: (bi * nq + i, 0)),
        out_shape=jax.ShapeDtypeStruct((b * s_len, xw), BF16),
        compiler_params=_params("parallel", "parallel"),
    )(q, k, v)


def kernel(x, mem, positions, norm_mix, w_in, conv_dw_w, conv_dw_b, conv_ln_g, conv_ln_b, sg_ln_g, sg_ln_b, sg_w, sg_b, sc_w, w_gate, b_gate, w_br, w_out, norm_xattn, norm_mem, xq_w, xk_w, xv_w, xo_w, norm_mlp, mlp_w1, mlp_w2, final_norm):
    b, s_len, d = x.shape
    n_mem = mem.shape[1]
    depth = w_in.shape[0]
    n_branch = w_gate.shape[1]
    bw = w_br.shape[1] // n_branch
    m = b * s_len
    assert bw % A_HEAD_DIM == 0 and IDX_WIDTH % bw == 0
    tq = min(256, s_len)

    tables = _rope_tables(positions)
    h = x.reshape(m, d)
    mem2 = mem.reshape(b * n_mem, d)

    o_qi = 3 * bw
    o_ki = o_qi + IDX_WIDTH
    o_wi = o_ki + IDX_DIM
    o_b = o_wi + IDX_HEADS
    col_a = IDX_WIDTH // bw
    col_b = col_a + 3
    col_c, col_d = col_b + 2, col_b + 4

    for l in range(depth):
        wl = w_in[l]
        w_big = jnp.concatenate([wl[:, o_qi:o_ki], wl[:, :o_qi], wl[:, o_b:]], axis=1).astype(BF16)
        w_small = jnp.concatenate(
            [wl[:, o_ki:o_wi], wl[:, o_ki:o_wi], wl[:, o_wi:o_b],
             jnp.zeros((d, SMALL_N - 2 * IDX_DIM - IDX_HEADS), F32)], axis=1).astype(BF16)

        xn = _rmsnorm(h, norm_mix[l], BF16)
        z = _in_proj(xn, w_big, tables, bw)
        kiwi = _in_proj_small(xn, w_small, tables)

        bias = _index_mask(z, kiwi, b, s_len, bw, tq)
        o_a = _masked_attention(z, bias, b, s_len, bw, col_a, tq)
        o_b_ = _conformer(z, b, s_len, bw, col_b, conv_dw_w[l], conv_dw_b[l], conv_ln_g[l], conv_ln_b[l])
        o_c = _gmlp(z, m, bw, col_c, sg_ln_g[l], sg_ln_b[l], sg_w[l], sg_b[l])
        o_d = _shortconv(z, b, s_len, bw, col_d, sc_w[l])
        o4 = jnp.stack([o_a, o_b_, o_c, o_d])

        y = _combine(xn, w_gate[l].astype(BF16), b_gate[l], o4,
                     w_br[l].reshape(n_branch, bw, d).astype(BF16))
        h = _matmul(y, w_out[l].astype(BF16), F32, res=h, tn=512)

        hn = _rmsnorm(h, norm_xattn[l], BF16)
        memn = _rmsnorm(mem2, norm_mem[l], BF16)
        qx = _matmul(hn, xq_w[l].astype(BF16), BF16)
        kx = _matmul(memn, xk_w[l].astype(BF16), BF16)
        vx = _matmul(memn, xv_w[l].astype(BF16), BF16)
        ox = _xattn(qx, kx, vx, b, s_len, n_mem)
        h = _matmul(ox, xo_w[l].astype(BF16), F32, res=h)

        hn = _rmsnorm(h, norm_mlp[l], BF16)
        a = _matmul(hn, mlp_w1[l].astype(BF16), BF16, epilogue="relu2")
        h = _matmul(a, mlp_w2[l].astype(BF16), F32, res=h, tk=2048)

    return _rmsnorm(h, final_norm, F32).reshape(b, s_len, d)
```

```python
import functools

import jax
import jax.numpy as jnp
from jax import lax
from jax.experimental import pallas as pl
from jax.experimental.pallas import tpu as pltpu

F32 = jnp.float32
BF16 = jnp.bfloat16
I32 = jnp.int32

EPS = 1e-6
ROPE_THETA = 500000.0
LANES = 128
A_HEAD_DIM = 128
A_ROPE_DIM = A_HEAD_DIM // 4
IDX_HEADS = 16
IDX_DIM = 64
IDX_WIDTH = IDX_HEADS * IDX_DIM
IDX_ROPE_DIM = IDX_DIM // 4
TOPK_MAX = 256
SG_CHUNK = 128
X_HEADS = 4
NEG = -1e30
INT_MIN = -2147483648
VMEM_LIMIT = 56 * 1024 * 1024
CONV_HALO = 32
SC_HALO = 8
SMALL_N = 2 * LANES
LOG2E = 1.4426950408889634
Q_PRESCALE = float(A_HEAD_DIM ** -0.5 * LOG2E)
N_BRANCH = 4
FLASH_SUB = 4


def _params(*sem):
    return pltpu.CompilerParams(dimension_semantics=sem, vmem_limit_bytes=VMEM_LIMIT)


def _dot(a, b):
    return jnp.dot(a, b, preferred_element_type=F32)


def _dot_nt(a, b):
    return lax.dot_general(a, b, (((1,), (1,)), ((), ())), preferred_element_type=F32)


def _rmsnorm_body(x_ref, g_ref, o_ref):
    x = x_ref[...]
    ms = jnp.mean(x * x, axis=-1, keepdims=True)
    o_ref[...] = ((x * lax.rsqrt(ms + EPS)) * g_ref[...]).astype(o_ref.dtype)


def _rmsnorm(x, g, out_dtype):
    m, d = x.shape
    tr = min(256, m)
    return pl.pallas_call(
        _rmsnorm_body,
        grid=(m // tr,),
        in_specs=[pl.BlockSpec((tr, d), lambda i: (i, 0)),
                  pl.BlockSpec((1, d), lambda i: (0, 0))],
        out_specs=pl.BlockSpec((tr, d), lambda i: (i, 0)),
        out_shape=jax.ShapeDtypeStruct((m, d), out_dtype),
        compiler_params=_params("parallel"),
    )(x, g.reshape(1, d))


def _mm_body(*refs, nk, epilogue, has_res):
    a_ref, b_ref = refs[0], refs[1]
    res_ref = refs[2] if has_res else None
    o_ref = refs[3] if has_res else refs[2]
    acc_ref = refs[-1] if nk > 1 else None

    def finish(acc):
        if epilogue == "relu2":
            r = jnp.maximum(acc, 0.0)
            acc = r * r
        if has_res:
            acc = res_ref[...] + acc
        o_ref[...] = acc.astype(o_ref.dtype)

    part = _dot(a_ref[...], b_ref[...])
    if nk == 1:
        finish(part)
        return
    k = pl.program_id(2)

    @pl.when(k == 0)
    def _():
        acc_ref[...] = part

    @pl.when(k > 0)
    def _():
        acc_ref[...] += part

    @pl.when(k == nk - 1)
    def _():
        finish(acc_ref[...])


def _matmul(a, b, out_dtype, res=None, epilogue=None, tm=1024, tn=1024, tk=4096):
    m, kdim = a.shape
    n = b.shape[1]
    tm, tn, tk = min(tm, m), min(tn, n), min(tk, kdim)
    nk = kdim // tk
    in_specs = [pl.BlockSpec((tm, tk), lambda i, j, k: (i, k)),
                pl.BlockSpec((tk, tn), lambda i, j, k: (k, j))]
    args = [a, b]
    if res is not None:
        in_specs.append(pl.BlockSpec((tm, tn), lambda i, j, k: (i, j)))
        args.append(res)
    return pl.pallas_call(
        functools.partial(_mm_body, nk=nk, epilogue=epilogue, has_res=res is not None),
        grid=(m // tm, n // tn, nk),
        in_specs=in_specs,
        out_specs=pl.BlockSpec((tm, tn), lambda i, j, k: (i, j)),
        out_shape=jax.ShapeDtypeStruct((m, n), out_dtype),
        scratch_shapes=[pltpu.VMEM((tm, tn), F32)] if nk > 1 else [],
        compiler_params=_params("parallel", "parallel", "arbitrary"),
    )(*args)


def _tables_body(pos_ref, inva_ref, invi_ref, ca_ref, sa_ref, ci_ref, si_ref):
    p = pos_ref[...]
    ang_a = p * inva_ref[...]
    ang_i = p * invi_ref[...]
    ca_ref[...] = jnp.cos(ang_a)
    sa_ref[...] = jnp.sin(ang_a)
    ci_ref[...] = jnp.cos(ang_i)
    si_ref[...] = jnp.sin(ang_i)


def _rope_tables(positions):
    b, s = positions.shape
    m = b * s
    pos = jnp.broadcast_to(positions.astype(F32).reshape(m, 1), (m, LANES))

    def inv_row(rot, period):
        inv = ROPE_THETA ** (-jnp.arange(0, rot, 2, dtype=F32) / rot)
        one = jnp.concatenate([inv, inv, jnp.zeros((period - rot,), F32)])
        return jnp.tile(one, LANES // period).reshape(1, LANES)

    tr = min(1024, m)
    spec = pl.BlockSpec((tr, LANES), lambda i: (i, 0))
    row = pl.BlockSpec((1, LANES), lambda i: (0, 0))
    sds = jax.ShapeDtypeStruct((m, LANES), F32)
    return pl.pallas_call(
        _tables_body,
        grid=(m // tr,),
        in_specs=[spec, row, row],
        out_specs=[spec] * 4,
        out_shape=[sds] * 4,
        compiler_params=_params("parallel"),
    )(pos, inv_row(A_ROPE_DIM, A_HEAD_DIM), inv_row(IDX_ROPE_DIM, IDX_DIM))


def _rope_group(x, cos, sin, first_half, half):
    rot = jnp.where(first_half, -pltpu.roll(x, LANES - half, 1), pltpu.roll(x, half, 1))
    return x * cos + sin * rot


def _first_half_mask(rows, period, half):
    lane = lax.broadcasted_iota(I32, (rows, LANES), 1)
    return (lane & (period - 1)) < half


def _win_body(a_ref, b_ref, ca_ref, sa_ref, ci_ref, si_ref, o_ref, *, n_qi):
    j = pl.program_id(1)
    acc = _dot(a_ref[...], b_ref[...])
    tm, tn = acc.shape

    def roped(cos_ref, sin_ref, period, half, mult=None):
        first = _first_half_mask(tm, period, half)
        cos, sin = cos_ref[...], sin_ref[...]
        for g in range(tn // LANES):
            sl = slice(g * LANES, (g + 1) * LANES)
            r = _rope_group(acc[:, sl], cos, sin, first, half)
            if mult is not None:
                r = r * mult
            o_ref[:, sl] = r.astype(o_ref.dtype)

    is_i = j < n_qi

    @pl.when(j == n_qi)
    def _():
        roped(ca_ref, sa_ref, A_HEAD_DIM, A_ROPE_DIM // 2, mult=Q_PRESCALE)

    @pl.when(j == n_qi + 1)
    def _():
        roped(ca_ref, sa_ref, A_HEAD_DIM, A_ROPE_DIM // 2)

    @pl.when(is_i)
    def _():
        roped(ci_ref, si_ref, IDX_DIM, IDX_ROPE_DIM // 2)

    @pl.when(j > n_qi + 1)
    def _():
        o_ref[...] = acc.astype(o_ref.dtype)


def _in_proj(xn, w_big, tables, bw):
    m, d = xn.shape
    n = w_big.shape[1]
    tm, tn = min(1024, m), bw
    ca, sa, ci, si = tables
    tab = pl.BlockSpec((tm, LANES), lambda i, j: (i, 0))
    return pl.pallas_call(
        functools.partial(_win_body, n_qi=IDX_WIDTH // bw),
        grid=(m // tm, n // tn),
        in_specs=[pl.BlockSpec((tm, d), lambda i, j: (i, 0)),
                  pl.BlockSpec((d, tn), lambda i, j: (0, j)),
                  tab, tab, tab, tab],
        out_specs=pl.BlockSpec((tm, tn), lambda i, j: (i, j)),
        out_shape=jax.ShapeDtypeStruct((m, n), BF16),
        compiler_params=_params("parallel", "parallel"),
    )(xn, w_big, ca, sa, ci, si)


def _wsmall_body(a_ref, b_ref, ci_ref, si_ref, o_ref):
    acc = _dot(a_ref[...], b_ref[...])
    first = _first_half_mask(acc.shape[0], IDX_DIM, IDX_ROPE_DIM // 2)
    o_ref[:, :LANES] = _rope_group(acc[:, :LANES], ci_ref[...], si_ref[...], first, IDX_ROPE_DIM // 2)
    o_ref[:, LANES:] = acc[:, LANES:]


def _in_proj_small(xn, w_small, tables):
    m, d = xn.shape
    tm = min(1024, m)
    tab = pl.BlockSpec((tm, LANES), lambda i: (i, 0))
    return pl.pallas_call(
        _wsmall_body,
        grid=(m // tm,),
        in_specs=[pl.BlockSpec((tm, d), lambda i: (i, 0)),
                  pl.BlockSpec((d, SMALL_N), lambda i: (0, 0)),
                  tab, tab],
        out_specs=pl.BlockSpec((tm, SMALL_N), lambda i: (i, 0)),
        out_shape=jax.ShapeDtypeStruct((m, SMALL_N), F32),
        compiler_params=_params("parallel"),
    )(xn, w_small, tables[2], tables[3])


def _sortable(score):
    bits = lax.bitcast_convert_type(score, I32)
    return jnp.where(bits < 0, bits ^ jnp.int32(0x7FFFFFFF), bits)


def _index_body(qi_ref, w_ref, ki_ref, bias_ref, kbf_scr, qm_scr, sc_scr, jthr_scr,
                *, tq, topk, scale, s_len):
    i = pl.program_id(1)
    n_chunks = s_len // tq
    n_live = i + 1

    @pl.when(i == 0)
    def _():
        kbf_scr[...] = ki_ref[...].astype(BF16)

    lane = lax.broadcasted_iota(I32, (tq, LANES), 1)
    for p in range(IDX_HEADS // 2):
        qp = qi_ref[:, p * LANES:(p + 1) * LANES]
        zero = jnp.zeros_like(qp)
        qm_scr[2 * p] = jnp.where(lane < IDX_DIM, qp, zero)
        qm_scr[2 * p + 1] = jnp.where(lane >= IDX_DIM, qp, zero)
    w_t = w_ref[...].T

    def chunk_rows(c):
        return pl.ds(pl.multiple_of(c * tq, tq), tq)

    def score_body(c, carry):
        rows = chunk_rows(c)
        kc = kbf_scr[rows, :]
        for h in range(0, IDX_HEADS, 2):
            rel = (jnp.maximum(_dot_nt(kc, qm_scr[h]), 0.0) * w_t[h:h + 1, :]
                   + jnp.maximum(_dot_nt(kc, qm_scr[h + 1]), 0.0) * w_t[h + 1:h + 2, :])
            if h == 0:
                sc_scr[rows, :] = rel
            else:
                sc_scr[rows, :] += rel
        return carry

    lax.fori_loop(0, n_live, score_body, 0)

    q_pos = i * tq + lax.broadcasted_iota(I32, (tq, tq), 1)
    k_off = lax.broadcasted_iota(I32, (tq, tq), 0)

    def key_body(c, carry):
        rows = chunk_rows(c)
        key = _sortable(sc_scr[rows, :] * scale)
        key = jnp.where(c * tq + k_off <= q_pos, key, jnp.int32(INT_MIN))
        sc_scr[rows, :] = lax.bitcast_convert_type(key, F32)
        return carry

    lax.fori_loop(0, n_live, key_body, 0)

    def load_keys(c):
        return lax.bitcast_convert_type(sc_scr[chunk_rows(c), :], I32)

    def count(pred):
        def body(c, cnt8):
            hit = pred(load_keys(c), c * tq + k_off).astype(I32)
            return cnt8 + hit.reshape(tq // 8, 8, tq).sum(axis=0)
        cnt8 = lax.fori_loop(0, n_live, body, jnp.zeros((8, tq), I32))
        return cnt8.sum(axis=0, keepdims=True)

    cnt0 = count(lambda key, idx: key >= 0)
    prefix0 = jnp.where(cnt0 >= topk, jnp.int32(0), jnp.int32(INT_MIN))

    def bit_body(it, prefix):
        cand = prefix | jnp.left_shift(jnp.int32(1), jnp.int32(30) - it)
        cnt = count(lambda key, idx: key >= cand)
        return jnp.where(cnt >= topk, cand, prefix)

    thr = lax.fori_loop(0, 31, bit_body, prefix0)

    cnt_gt = count(lambda key, idx: key > thr)
    cnt_ge = count(lambda key, idx: key >= thr)
    need = topk - cnt_gt
    tie = jnp.logical_and(cnt_ge > topk, thr != INT_MIN)
    jthr_scr[...] = jnp.full((8, tq), s_len, I32)

    @pl.when(jnp.max(tie.astype(I32)) > 0)
    def _():
        n_bits = max(1, (s_len - 1).bit_length())

        def idx_body(it, ans):
            cand = ans | jnp.left_shift(jnp.int32(1), jnp.int32(n_bits - 1) - it)
            cnt = count(lambda key, idx: jnp.logical_and(key == thr, idx < cand))
            return jnp.where(cnt < need, cand, ans)

        ans = lax.fori_loop(0, n_bits, idx_body, jnp.zeros((1, tq), I32))
        jthr_scr[...] = jnp.broadcast_to(jnp.where(tie, ans, s_len), (8, tq))

    jthr = jthr_scr[0:1, :]

    def out_body(c, carry):
        key = load_keys(c)
        idx = c * tq + k_off
        sel = jnp.logical_or(key > thr, jnp.logical_and(key == thr, idx <= jthr))
        sel = jnp.logical_and(sel, idx <= q_pos)
        bias_t = jnp.where(sel, 0.0, NEG)
        bias_ref[0, 0, c] = bias_t.T.astype(bias_ref.dtype)
        return carry

    lax.fori_loop(0, n_live, out_body, 0)

    def fill_body(c, carry):
        bias_ref[0, 0, c] = jnp.full((tq, tq), NEG, bias_ref.dtype)
        return carry

    lax.fori_loop(n_live, n_chunks, fill_body, 0)


def _index_mask(z, kiwi, b, s_len, bw, tq):
    nq = s_len // tq
    topk = min(TOPK_MAX, s_len // 4)
    qi_col = 0
    return pl.pallas_call(
        functools.partial(_index_body, tq=tq, topk=topk, s_len=s_len,
                          scale=float((IDX_DIM * IDX_HEADS) ** -0.5)),
        grid=(b, nq),
        in_specs=[pl.BlockSpec((tq, IDX_WIDTH), lambda bi, i: (bi * nq + i, qi_col)),
                  pl.BlockSpec((tq, LANES), lambda bi, i: (bi * nq + i, 1)),
                  pl.BlockSpec((s_len, LANES), lambda bi, i: (bi, 0))],
        out_specs=pl.BlockSpec((1, 1, nq, tq, tq), lambda bi, i: (bi, i, 0, 0, 0)),
        out_shape=jax.ShapeDtypeStruct((b, nq, nq, tq, tq), BF16),
        scratch_shapes=[pltpu.VMEM((s_len, LANES), BF16),
                        pltpu.VMEM((IDX_HEADS, tq, LANES), BF16),
                        pltpu.VMEM((s_len, tq), F32),
                        pltpu.VMEM((8, tq), I32)],
        compiler_params=_params("arbitrary", "arbitrary"),
    )(z, kiwi, kiwi)


def _flash_body(q_ref, k_ref, v_ref, b_ref, o_ref, m_scr, l_scr, acc_scr, *, heads, nk, sub):
    i, j = pl.program_id(1), pl.program_id(2)
    tk = k_ref.shape[0]

    @pl.when(j == 0)
    def _():
        m_scr[...] = jnp.full(m_scr.shape, NEG, F32)
        l_scr[...] = jnp.zeros(l_scr.shape, F32)
        acc_scr[...] = jnp.zeros(acc_scr.shape, F32)

    @pl.when(j * sub <= i)
    def _():
        bias = jnp.concatenate([b_ref[0, 0, c] for c in range(sub)], axis=1).astype(F32)
        for h in range(heads):
            sl = slice(h * A_HEAD_DIM, (h + 1) * A_HEAD_DIM)
            s = _dot_nt(q_ref[:, sl], k_ref[:, sl]) + bias
            m_prev = m_scr[h]
            m_new = jnp.maximum(m_prev, jnp.max(s, axis=1, keepdims=True))
            alpha = jnp.exp2(m_prev - m_new)
            p = jnp.exp2(s - jnp.tile(m_new, (1, tk // LANES)))
            l_scr[h] = alpha * l_scr[h] + jnp.sum(p, axis=1, keepdims=True)
            acc_scr[:, sl] = alpha * acc_scr[:, sl] + _dot(p.astype(BF16), v_ref[:, sl])
            m_scr[h] = m_new

    @pl.when(j == nk - 1)
    def _():
        for h in range(heads):
            sl = slice(h * A_HEAD_DIM, (h + 1) * A_HEAD_DIM)
            o_ref[0, :, sl] = (acc_scr[:, sl] / l_scr[h]).astype(o_ref.dtype)


def _masked_attention(z, bias, b, s_len, bw, col, tq):
    nq = s_len // tq
    sub = min(FLASH_SUB, nq)
    nk = nq // sub
    tk = sub * tq
    heads = bw // A_HEAD_DIM
    kblk = lambda bi, i, j: bi * nk + jnp.minimum(j, i // sub)
    return pl.pallas_call(
        functools.partial(_flash_body, heads=heads, nk=nk, sub=sub),
        grid=(b, nq, nk),
        in_specs=[pl.BlockSpec((tq, bw), lambda bi, i, j: (bi * nq + i, col)),
                  pl.BlockSpec((tk, bw), lambda bi, i, j: (kblk(bi, i, j), col + 1)),
                  pl.BlockSpec((tk, bw), lambda bi, i, j: (kblk(bi, i, j), col + 2)),
                  pl.BlockSpec((1, 1, sub, tq, tq),
                               lambda bi, i, j: (bi, i, jnp.minimum(j, i // sub), 0, 0))],
        out_specs=pl.BlockSpec((1, tq, bw), lambda bi, i, j: (0, bi * nq + i, 0)),
        out_shape=jax.ShapeDtypeStruct((N_BRANCH, b * s_len, bw), BF16),
        scratch_shapes=[pltpu.VMEM((heads, tq, LANES), F32),
                        pltpu.VMEM((heads, tq, LANES), F32),
                        pltpu.VMEM((tq, bw), F32)],
        compiler_params=_params("parallel", "parallel", "arbitrary"),
    )(z, z, z, bias)


def _slab_out(slabs, slab, rows, bw, index_map):
    return dict(
        out_specs=pl.BlockSpec((1, rows, bw), lambda *g: (slab,) + tuple(index_map(*g))),
        out_shape=jax.ShapeDtypeStruct(slabs.shape, slabs.dtype),
    )


def _conformer_body(a_ref, g_ref, ah_ref, gh_ref, w_ref, bdw_ref, lng_ref, lnb_ref, slabs_ref,
                    o_ref, hbuf, ybuf, *, width):
    del slabs_ref
    i = pl.program_id(1)
    ts, bw = a_ref.shape
    ncg = bw // LANES
    h_main = a_ref[...].astype(F32) * jax.nn.sigmoid(g_ref[...].astype(F32))
    h_halo = ah_ref[...].astype(F32) * jax.nn.sigmoid(gh_ref[...].astype(F32))
    h_halo = jnp.where(i == 0, 0.0, h_halo)
    for c in range(ncg):
        sl = slice(c * LANES, (c + 1) * LANES)
        hbuf[c, 0:CONV_HALO, :] = h_halo[:, sl]
        hbuf[c, CONV_HALO:, :] = h_main[:, sl]

    def col_body(c, carry):
        wc = w_ref[c]
        acc = jnp.zeros((ts, LANES), F32)
        for k in range(width):
            off = CONV_HALO - (width - 1) + k
            acc = acc + hbuf[c, pl.ds(off, ts), :] * wc[k:k + 1, :]
        ybuf[c] = acc + bdw_ref[c]
        return carry

    lax.fori_loop(0, ncg, col_body, 0)

    tot = ybuf[0]
    for c in range(1, ncg):
        tot = tot + ybuf[c]
    mu = jnp.sum(tot, axis=1, keepdims=True) / bw
    sq = jnp.zeros((ts, LANES), F32)
    for c in range(ncg):
        d = ybuf[c] - mu
        sq = sq + d * d
    inv = lax.rsqrt(jnp.sum(sq, axis=1, keepdims=True) / bw + EPS)
    for c in range(ncg):
        y = (ybuf[c] - mu) * inv * lng_ref[c] + lnb_ref[c]
        o_ref[0, :, c * LANES:(c + 1) * LANES] = (y * jax.nn.sigmoid(y)).astype(o_ref.dtype)


def _lane_groups(v):
    return v.reshape(-1, 1, LANES)


ANY_SPEC = pl.BlockSpec(memory_space=pl.ANY)


def _conformer(z, slabs, b, s_len, bw, col, w_dw, b_dw, ln_g, ln_b, ts=256):
    width = w_dw.shape[0]
    ncg = bw // LANES
    ns = s_len // ts
    hb = ts // CONV_HALO
    w_cg = w_dw.reshape(width, ncg, LANES).transpose(1, 0, 2)
    main = lambda cb: pl.BlockSpec((ts, bw), lambda bi, i: (bi * ns + i, cb))
    halo = lambda cb: pl.BlockSpec(
        (CONV_HALO, bw), lambda bi, i: ((bi * ns + i) * hb - jnp.minimum(i, 1), cb))
    small = pl.BlockSpec((ncg, 1, LANES), lambda bi, i: (0, 0, 0))
    return pl.pallas_call(
        functools.partial(_conformer_body, width=width),
        grid=(b, ns),
        in_specs=[main(col), main(col + 1), halo(col), halo(col + 1),
                  pl.BlockSpec((ncg, width, LANES), lambda bi, i: (0, 0, 0)),
                  small, small, small, ANY_SPEC],
        input_output_aliases={8: 0},
        scratch_shapes=[pltpu.VMEM((ncg, ts + CONV_HALO, LANES), F32),
                        pltpu.VMEM((ncg, ts, LANES), F32)],
        compiler_params=_params("parallel", "parallel"),
        **_slab_out(slabs, 1, ts, bw, lambda bi, i: (bi * ns + i, 0)),
    )(z, z, z, z, w_cg, _lane_groups(b_dw), _lane_groups(ln_g), _lane_groups(ln_b), slabs)


def _gmlp_body(u_ref, v_ref, lng_ref, lnb_ref, w_ref, bias_ref, slabs_ref, o_ref):
    del slabs_ref
    ts, bw = u_ref.shape
    groups = bw // LANES
    u = jax.nn.gelu(u_ref[...].astype(F32))
    v = jax.nn.gelu(v_ref[...].astype(F32))
    mu = jnp.mean(v, axis=-1, keepdims=True)
    vc = v - mu
    var = jnp.mean(vc * vc, axis=-1, keepdims=True)
    vn = (vc * lax.rsqrt(var + EPS) * lng_ref[...] + lnb_ref[...]).astype(BF16)
    row = lax.broadcasted_iota(I32, (SG_CHUNK, SG_CHUNK), 0)
    col = lax.broadcasted_iota(I32, (SG_CHUNK, SG_CHUNK), 1)
    for g in range(groups):
        cs = slice(g * LANES, (g + 1) * LANES)
        wg = jnp.where(row >= col, w_ref[g], 0.0).astype(BF16)
        for c in range(ts // SG_CHUNK):
            rs = slice(c * SG_CHUNK, (c + 1) * SG_CHUNK)
            mixed = _dot(wg, vn[rs, cs]) + bias_ref[:, cs]
            o_ref[0, rs, cs] = (u[rs, cs] * mixed).astype(o_ref.dtype)


def _gmlp(z, slabs, m, bw, col, ln_g, ln_b, sg_w, sg_b, ts=256):
    groups = bw // LANES
    bias = jnp.broadcast_to(sg_b.T[:, :, None], (SG_CHUNK, groups, LANES)).reshape(SG_CHUNK, bw)
    row = pl.BlockSpec((1, bw), lambda i: (0, 0))
    return pl.pallas_call(
        _gmlp_body,
        grid=(m // ts,),
        in_specs=[pl.BlockSpec((ts, bw), lambda i: (i, col)),
                  pl.BlockSpec((ts, bw), lambda i: (i, col + 1)),
                  row, row,
                  pl.BlockSpec((groups, SG_CHUNK, SG_CHUNK), lambda i: (0, 0, 0)),
                  pl.BlockSpec((SG_CHUNK, bw), lambda i: (0, 0)), ANY_SPEC],
        input_output_aliases={6: 0},
        compiler_params=_params("parallel"),
        **_slab_out(slabs, 2, ts, bw, lambda i: (i, 0)),
    )(z, z, ln_g.reshape(1, bw), ln_b.reshape(1, bw), sg_w, bias, slabs)


def _shortconv_body(bg_ref, cg_ref, h_ref, cgh_ref, hh_ref, w_ref, slabs_ref, o_ref, xbuf, *, width):
    del slabs_ref
    i = pl.program_id(1)
    ts = bg_ref.shape[0]
    x_halo = cgh_ref[...].astype(F32) * hh_ref[...].astype(F32)
    xbuf[0:SC_HALO, :] = jnp.where(i == 0, 0.0, x_halo)
    xbuf[SC_HALO:, :] = cg_ref[...].astype(F32) * h_ref[...].astype(F32)
    acc = jnp.zeros(bg_ref.shape, F32)
    for k in range(width):
        off = SC_HALO - (width - 1) + k
        acc = acc + xbuf[pl.ds(off, ts), :] * w_ref[k:k + 1, :]
    o_ref[0] = (bg_ref[...].astype(F32) * acc).astype(o_ref.dtype)


def _shortconv(z, slabs, b, s_len, bw, col, w_conv, ts=256):
    width = w_conv.shape[0]
    ns = s_len // ts
    hb = ts // SC_HALO
    main = lambda cb: pl.BlockSpec((ts, bw), lambda bi, i: (bi * ns + i, cb))
    halo = lambda cb: pl.BlockSpec(
        (SC_HALO, bw), lambda bi, i: ((bi * ns + i) * hb - jnp.minimum(i, 1), cb))
    return pl.pallas_call(
        functools.partial(_shortconv_body, width=width),
        grid=(b, ns),
        in_specs=[main(col), main(col + 1), main(col + 2), halo(col + 1), halo(col + 2),
                  pl.BlockSpec((width, bw), lambda bi, i: (0, 0)), ANY_SPEC],
        input_output_aliases={6: 0},
        scratch_shapes=[pltpu.VMEM((ts + SC_HALO, bw), F32)],
        compiler_params=_params("parallel", "parallel"),
        **_slab_out(slabs, 3, ts, bw, lambda bi, i: (bi * ns + i, 0)),
    )(z, z, z, z, z, w_conv, slabs)


def _combine_body(xn_ref, wg_ref, bg_ref, o_ref, wbr_ref, y_ref, acc_ref, *, n_branch):
    n = pl.program_id(2)
    gate = jax.nn.sigmoid(_dot(xn_ref[...], wg_ref[0]) + bg_ref[0])
    term = gate * _dot(o_ref[0], wbr_ref[0])

    @pl.when(n == 0)
    def _():
        acc_ref[...] = term

    @pl.when(n > 0)
    def _():
        acc_ref[...] += term

    @pl.when(n == n_branch - 1)
    def _():
        y_ref[...] = acc_ref[...].astype(y_ref.dtype)


def _combine(xn, w_gate, b_gate, o4, w_br4, tm=1024, tn=512):
    m, d = xn.shape
    n_branch, bw = o4.shape[0], o4.shape[2]
    tm, tn = min(tm, m), min(tn, d)
    return pl.pallas_call(
        functools.partial(_combine_body, n_branch=n_branch),
        grid=(m // tm, d // tn, n_branch),
        in_specs=[pl.BlockSpec((tm, d), lambda i, j, n: (i, 0)),
                  pl.BlockSpec((1, d, tn), lambda i, j, n: (n, 0, j)),
                  pl.BlockSpec((1, 1, tn), lambda i, j, n: (n, 0, j)),
                  pl.BlockSpec((1, tm, bw), lambda i, j, n: (n, i, 0)),
                  pl.BlockSpec((1, bw, tn), lambda i, j, n: (n, 0, j))],
        out_specs=pl.BlockSpec((tm, tn), lambda i, j, n: (i, j)),
        out_shape=jax.ShapeDtypeStruct((m, d), BF16),
        scratch_shapes=[pltpu.VMEM((tm, tn), F32)],
        compiler_params=_params("parallel", "parallel", "arbitrary"),
    )(xn, w_gate, b_gate.reshape(n_branch, 1, d), o4, w_br4)


def _xattn_body(q_ref, k_ref, v_ref, o_ref, *, heads, scale):
    hd = q_ref.shape[1] // heads
    for h in range(heads):
        sl = slice(h * hd, (h + 1) * hd)
        s = _dot_nt(q_ref[:, sl], k_ref[:, sl]) * scale
        p = jnp.exp(s - jnp.max(s, axis=1, keepdims=True))
        l = jnp.sum(p, axis=1, keepdims=True)
        o_ref[:, sl] = (_dot(p.astype(BF16), v_ref[:, sl]) / l).astype(o_ref.dtype)


def _xattn(q, k, v, b, s_len, n_mem, tq=512):
    xw = q.shape[1]
    tq = min(tq, s_len)
    nq = s_len // tq
    kv = pl.BlockSpec((n_mem, xw), lambda bi, i: (bi, 0))
    return pl.pallas_call(
        functools.partial(_xattn_body, heads=X_HEADS, scale=float((xw // X_HEADS) ** -0.5)),
        grid=(b, nq),
        in_specs=[pl.BlockSpec((tq, xw), lambda bi, i: (bi * nq + i, 0)), kv, kv],
        out_specs=pl.BlockSpec((tq, xw), lambda bi, i: (bi * nq + i, 0)),
        out_shape=jax.ShapeDtypeStruct((b * s_len, xw), BF16),
        compiler_params=_params("parallel", "parallel"),
    )(q, k, v)


def kernel(x, mem, positions, norm_mix, w_in, conv_dw_w, conv_dw_b, conv_ln_g, conv_ln_b, sg_ln_g, sg_ln_b, sg_w, sg_b, sc_w, w_gate, b_gate, w_br, w_out, norm_xattn, norm_mem, xq_w, xk_w, xv_w, xo_w, norm_mlp, mlp_w1, mlp_w2, final_norm):
    b, s_len, d = x.shape
    n_mem = mem.shape[1]
    depth = w_in.shape[0]
    n_branch = w_gate.shape[1]
    bw = w_br.shape[1] // n_branch
    m = b * s_len
    assert bw % A_HEAD_DIM == 0 and IDX_WIDTH % bw == 0
    tq = min(256, s_len)

    tables = _rope_tables(positions)
    h = x.reshape(m, d)
    mem2 = mem.reshape(b * n_mem, d)

    o_qi = 3 * bw
    o_ki = o_qi + IDX_WIDTH
    o_wi = o_ki + IDX_DIM
    o_b = o_wi + IDX_HEADS
    col_a = IDX_WIDTH // bw
    col_b = col_a + 3
    col_c, col_d = col_b + 2, col_b + 4

    for l in range(depth):
        wl = w_in[l]
        w_big = jnp.concatenate([wl[:, o_qi:o_ki], wl[:, :o_qi], wl[:, o_b:]], axis=1).astype(BF16)
        w_small = jnp.concatenate(
            [wl[:, o_ki:o_wi], wl[:, o_ki:o_wi], wl[:, o_wi:o_b],
             jnp.zeros((d, SMALL_N - 2 * IDX_DIM - IDX_HEADS), F32)], axis=1).astype(BF16)

        xn = _rmsnorm(h, norm_mix[l], BF16)
        z = _in_proj(xn, w_big, tables, bw)
        kiwi = _in_proj_small(xn, w_small, tables)

        bias = _index_mask(z, kiwi, b, s_len, bw, tq)
        o4 = _masked_attention(z, bias, b, s_len, bw, col_a, tq)
        o4 = _conformer(z, o4, b, s_len, bw, col_b, conv_dw_w[l], conv_dw_b[l], conv_ln_g[l], conv_ln_b[l])
        o4 = _gmlp(z, o4, m, bw, col_c, sg_ln_g[l], sg_ln_b[l], sg_w[l], sg_b[l])
        o4 = _shortconv(z, o4, b, s_len, bw, col_d, sc_w[l])

        y = _combine(xn, w_gate[l].astype(BF16), b_gate[l], o4,
                     w_br[l].reshape(n_branch, bw, d).astype(BF16))
        h = _matmul(y, w_out[l].astype(BF16), F32, res=h, tn=512)

        hn = _rmsnorm(h, norm_xattn[l], BF16)
        memn = _rmsnorm(mem2, norm_mem[l], BF16)
        qx = _matmul(hn, xq_w[l].astype(BF16), BF16)
        kx = _matmul(memn, xk_w[l].astype(BF16), BF16)
        vx = _matmul(memn, xv_w[l].astype(BF16), BF16)
        ox = _xattn(qx, kx, vx, b, s_len, n_mem)
        h = _matmul(ox, xo_w[l].astype(BF16), F32, res=h)

        hn = _rmsnorm(h, norm_mlp[l], BF16)
        a = _matmul(hn, mlp_w1[l].astype(BF16), BF16, epilogue="relu2")
        h = _matmul(a, mlp_w2[l].astype(BF16), F32, res=h, tk=2048)

    return _rmsnorm(h, final_norm, F32).reshape(b, s_len, d)
```

```python
import functools

import jax
import jax.numpy as jnp
from jax import lax
from jax.experimental import pallas as pl
from jax.experimental.pallas import tpu as pltpu

F32 = jnp.float32
BF16 = jnp.bfloat16
I32 = jnp.int32

EPS = 1e-6
ROPE_THETA = 500000.0
LANES = 128
A_HEAD_DIM = 128
A_ROPE_DIM = A_HEAD_DIM // 4
IDX_HEADS = 16
IDX_DIM = 64
IDX_WIDTH = IDX_HEADS * IDX_DIM
IDX_ROPE_DIM = IDX_DIM // 4
TOPK_MAX = 256
SG_CHUNK = 128
X_HEADS = 4
NEG = -1e30
INT_MIN = -2147483648
VMEM_LIMIT = 56 * 1024 * 1024
VMEM_LIMIT_MAX = 60 * 1024 * 1024
CONV_HALO = 32
SC_HALO = 8
SMALL_N = 2 * LANES
LOG2E = 1.4426950408889634
Q_PRESCALE = float(A_HEAD_DIM ** -0.5 * LOG2E)
N_BRANCH = 4
FLASH_SUB = 4


def _params(*sem, vmem_limit=VMEM_LIMIT):
    return pltpu.CompilerParams(dimension_semantics=sem, vmem_limit_bytes=vmem_limit)


def _dot(a, b):
    return jnp.dot(a, b, preferred_element_type=F32)


def _dot_nt(a, b):
    return lax.dot_general(a, b, (((1,), (1,)), ((), ())), preferred_element_type=F32)


def _rmsnorm_body(x_ref, g_ref, o_ref):
    x = x_ref[...]
    ms = jnp.mean(x * x, axis=-1, keepdims=True)
    o_ref[...] = ((x * lax.rsqrt(ms + EPS)) * g_ref[...]).astype(o_ref.dtype)


def _rmsnorm(x, g, out_dtype):
    m, d = x.shape
    tr = min(256, m)
    return pl.pallas_call(
        _rmsnorm_body,
        grid=(m // tr,),
        in_specs=[pl.BlockSpec((tr, d), lambda i: (i, 0)),
                  pl.BlockSpec((1, d), lambda i: (0, 0))],
        out_specs=pl.BlockSpec((tr, d), lambda i: (i, 0)),
        out_shape=jax.ShapeDtypeStruct((m, d), out_dtype),
        compiler_params=_params("parallel"),
    )(x, g.reshape(1, d))


def _mm_body(*refs, nk, epilogue, has_res):
    a_ref, b_ref = refs[0], refs[1]
    res_ref = refs[2] if has_res else None
    o_ref = refs[-1]

    part = _dot(a_ref[...], b_ref[0])
    if nk == 1:
        if epilogue == "relu2":
            r = jnp.maximum(part, 0.0)
            part = r * r
        if has_res:
            part = res_ref[...] + part
        o_ref[...] = part.astype(o_ref.dtype)
        return

    k = pl.program_id(2)

    @pl.when(k == 0)
    def _():
        o_ref[...] = res_ref[...] + part if has_res else part

    @pl.when(k > 0)
    def _():
        o_ref[...] += part


def _matmul(a, w, layer, out_dtype, res=None, epilogue=None, tm=1024, tn=1024, tk=4096,
            vmem_limit=VMEM_LIMIT):
    m, kdim = a.shape
    n = w.shape[2]
    tm, tn, tk = min(tm, m), min(tn, n), min(tk, kdim)
    nk = kdim // tk
    assert nk == 1 or (out_dtype == F32 and epilogue is None)
    in_specs = [pl.BlockSpec((tm, tk), lambda i, j, k: (i, k)),
                pl.BlockSpec((1, tk, tn), lambda i, j, k: (layer, k, j))]
    args = [a, w]
    if res is not None:
        in_specs.append(pl.BlockSpec((tm, tn), lambda i, j, k: (i, j)))
        args.append(res)
    return pl.pallas_call(
        functools.partial(_mm_body, nk=nk, epilogue=epilogue, has_res=res is not None),
        grid=(m // tm, n // tn, nk),
        in_specs=in_specs,
        out_specs=pl.BlockSpec((tm, tn), lambda i, j, k: (i, j)),
        out_shape=jax.ShapeDtypeStruct((m, n), out_dtype),
        compiler_params=_params("parallel", "parallel", "arbitrary", vmem_limit=vmem_limit),
    )(*args)


def _tables_body(pos_ref, inva_ref, invi_ref, ca_ref, sa_ref, ci_ref, si_ref):
    p = pos_ref[...]
    ang_a = p * inva_ref[...]
    ang_i = p * invi_ref[...]
    ca_ref[...] = jnp.cos(ang_a)
    sa_ref[...] = jnp.sin(ang_a)
    ci_ref[...] = jnp.cos(ang_i)
    si_ref[...] = jnp.sin(ang_i)


def _rope_tables(positions):
    b, s = positions.shape
    m = b * s
    pos = jnp.broadcast_to(positions.astype(F32).reshape(m, 1), (m, LANES))

    def inv_row(rot, period):
        inv = ROPE_THETA ** (-jnp.arange(0, rot, 2, dtype=F32) / rot)
        one = jnp.concatenate([inv, inv, jnp.zeros((period - rot,), F32)])
        return jnp.tile(one, LANES // period).reshape(1, LANES)

    tr = min(1024, m)
    spec = pl.BlockSpec((tr, LANES), lambda i: (i, 0))
    row = pl.BlockSpec((1, LANES), lambda i: (0, 0))
    sds = jax.ShapeDtypeStruct((m, LANES), F32)
    return pl.pallas_call(
        _tables_body,
        grid=(m // tr,),
        in_specs=[spec, row, row],
        out_specs=[spec] * 4,
        out_shape=[sds] * 4,
        compiler_params=_params("parallel"),
    )(pos, inv_row(A_ROPE_DIM, A_HEAD_DIM), inv_row(IDX_ROPE_DIM, IDX_DIM))


def _rope_group(x, cos, sin, first_half, half):
    rot = jnp.where(first_half, -pltpu.roll(x, LANES - half, 1), pltpu.roll(x, half, 1))
    return x * cos + sin * rot


def _first_half_mask(rows, period, half):
    lane = lax.broadcasted_iota(I32, (rows, LANES), 1)
    return (lane & (period - 1)) < half


def _rope_mm_body(a_ref, b_ref, cos_ref, sin_ref, o_ref, *, period, half, n_roped, scale_first):
    j = pl.program_id(1)
    acc = _dot(a_ref[...], b_ref[0])
    tm, tn = acc.shape
    first = _first_half_mask(tm, period, half)
    cos, sin = cos_ref[...], sin_ref[...]
    if n_roped is not None:
        roped = j < n_roped
        cos = jnp.where(roped, cos, 1.0)
        sin = jnp.where(roped, sin, 0.0)
    if scale_first:
        cos = cos * jnp.where(j == 0, Q_PRESCALE, 1.0)
        sin = sin * jnp.where(j == 0, Q_PRESCALE, 1.0)
    for g in range(tn // LANES):
        sl = slice(g * LANES, (g + 1) * LANES)
        o_ref[:, sl] = _rope_group(acc[:, sl], cos, sin, first, half).astype(o_ref.dtype)


def _rope_mm(xn, w, layer, col0, n_blocks, tn, cos, sin, out_dtype, *, period, half,
             n_roped=None, scale_first=False):
    m, d = xn.shape
    tm = min(1024, m)
    tab = pl.BlockSpec((tm, LANES), lambda i, j: (i, 0))
    return pl.pallas_call(
        functools.partial(_rope_mm_body, period=period, half=half, n_roped=n_roped,
                          scale_first=scale_first),
        grid=(m // tm, n_blocks),
        in_specs=[pl.BlockSpec((tm, d), lambda i, j: (i, 0)),
                  pl.BlockSpec((1, d, tn), lambda i, j: (layer, 0, col0 + j)),
                  tab, tab],
        out_specs=pl.BlockSpec((tm, tn), lambda i, j: (i, j)),
        out_shape=jax.ShapeDtypeStruct((m, n_blocks * tn), out_dtype),
        compiler_params=_params("parallel", "parallel"),
    )(xn, w, cos, sin)


def _sortable(score):
    bits = lax.bitcast_convert_type(score, I32)
    return jnp.where(bits < 0, bits ^ jnp.int32(0x7FFFFFFF), bits)


def _index_body(qi_ref, w_ref, ki_ref, bias_ref, kbf_scr, qm_scr, sc_scr, jthr_scr,
                *, tq, topk, scale, s_len):
    i = pl.program_id(1)
    n_chunks = s_len // tq
    n_live = i + 1

    @pl.when(i == 0)
    def _():
        kbf_scr[...] = ki_ref[...].astype(BF16)

    lane = lax.broadcasted_iota(I32, (tq, LANES), 1)
    for p in range(IDX_HEADS // 2):
        qp = qi_ref[:, p * LANES:(p + 1) * LANES]
        zero = jnp.zeros_like(qp)
        qm_scr[2 * p] = jnp.where(lane < IDX_DIM, qp, zero)
        qm_scr[2 * p + 1] = jnp.where(lane >= IDX_DIM, qp, zero)
    w_t = w_ref[...].T

    def chunk_rows(c):
        return pl.ds(pl.multiple_of(c * tq, tq), tq)

    def score_body(c, carry):
        rows = chunk_rows(c)
        kc = kbf_scr[rows, :]
        for h in range(0, IDX_HEADS, 2):
            rel = (jnp.maximum(_dot_nt(kc, qm_scr[h]), 0.0) * w_t[h:h + 1, :]
                   + jnp.maximum(_dot_nt(kc, qm_scr[h + 1]), 0.0) * w_t[h + 1:h + 2, :])
            if h == 0:
                sc_scr[rows, :] = rel
            else:
                sc_scr[rows, :] += rel
        return carry

    lax.fori_loop(0, n_live, score_body, 0)

    q_pos = i * tq + lax.broadcasted_iota(I32, (tq, tq), 1)
    k_off = lax.broadcasted_iota(I32, (tq, tq), 0)

    def key_body(c, carry):
        rows = chunk_rows(c)
        key = _sortable(sc_scr[rows, :] * scale)
        key = jnp.where(c * tq + k_off <= q_pos, key, jnp.int32(INT_MIN))
        sc_scr[rows, :] = lax.bitcast_convert_type(key, F32)
        return carry

    lax.fori_loop(0, n_live, key_body, 0)

    def load_keys(c):
        return lax.bitcast_convert_type(sc_scr[chunk_rows(c), :], I32)

    def count(pred):
        def hits(c):
            hit = pred(load_keys(c), c * tq + k_off).astype(I32)
            return hit.reshape(tq // 8, 8, tq).sum(axis=0)

        def pair_body(g, cnt8):
            return cnt8 + hits(2 * g) + hits(2 * g + 1)

        cnt8 = lax.fori_loop(0, jnp.right_shift(n_live, 1), pair_body, jnp.zeros((8, tq), I32))
        cnt8 = cnt8 + jnp.where((n_live & 1) == 1, hits(n_live - 1), 0)
        return cnt8.sum(axis=0, keepdims=True)

    full = q_pos[0:1, :] + 1 >= topk
    cnt0 = count(lambda key, idx: key >= 0)
    nonneg = cnt0 >= topk
    prefix0 = jnp.where(nonneg, jnp.int32(0), jnp.int32(INT_MIN))
    cntp0 = jnp.where(nonneg, cnt0, jnp.int32(s_len))

    def n_open(cntp):
        return jnp.max(jnp.where(jnp.logical_and(full, cntp != topk), 1, 0))

    def bit_cond(carry):
        it, _, _, still_open = carry
        return jnp.logical_and(it < 31, still_open > 0)

    def bit_body(carry):
        it, prefix, cntp, _ = carry
        cand = prefix | jnp.left_shift(jnp.int32(1), jnp.int32(30) - it)
        cnt = count(lambda key, idx: key >= cand)
        take = cnt >= topk
        cntp = jnp.where(take, cnt, cntp)
        return it + 1, jnp.where(take, cand, prefix), cntp, n_open(cntp)

    _, thr, cnt_ge, _ = lax.while_loop(bit_cond, bit_body,
                                       (jnp.int32(0), prefix0, cntp0, n_open(cntp0)))

    tie = jnp.logical_and(cnt_ge > topk, thr != INT_MIN)
    jthr_scr[...] = jnp.full((8, tq), s_len, I32)

    @pl.when(jnp.max(tie.astype(I32)) > 0)
    def _():
        n_bits = max(1, (s_len - 1).bit_length())
        need = topk - count(lambda key, idx: key > thr)

        def idx_body(it, ans):
            cand = ans | jnp.left_shift(jnp.int32(1), jnp.int32(n_bits - 1) - it)
            cnt = count(lambda key, idx: jnp.logical_and(key == thr, idx < cand))
            return jnp.where(cnt < need, cand, ans)

        ans = lax.fori_loop(0, n_bits, idx_body, jnp.zeros((1, tq), I32))
        jthr_scr[...] = jnp.broadcast_to(jnp.where(tie, ans, s_len), (8, tq))

    jthr = jthr_scr[0:1, :]

    def out_body(c, carry):
        key = load_keys(c)
        idx = c * tq + k_off
        sel = jnp.logical_or(key > thr, jnp.logical_and(key == thr, idx <= jthr))
        sel = jnp.logical_and(sel, idx <= q_pos)
        bias_t = jnp.where(sel, 0.0, NEG)
        bias_ref[0, 0, c] = bias_t.T.astype(bias_ref.dtype)
        return carry

    lax.fori_loop(0, n_live, out_body, 0)

    def fill_body(c, carry):
        bias_ref[0, 0, c] = jnp.full((tq, tq), NEG, bias_ref.dtype)
        return carry

    lax.fori_loop(n_live, n_chunks, fill_body, 0)


def _index_mask(zi, kiwi, b, s_len, tq):
    nq = s_len // tq
    topk = min(TOPK_MAX, s_len // 4)
    return pl.pallas_call(
        functools.partial(_index_body, tq=tq, topk=topk, s_len=s_len,
                          scale=float((IDX_DIM * IDX_HEADS) ** -0.5)),
        grid=(b, nq),
        in_specs=[pl.BlockSpec((tq, IDX_WIDTH), lambda bi, i: (bi * nq + i, 0)),
                  pl.BlockSpec((tq, LANES), lambda bi, i: (bi * nq + i, 1)),
                  pl.BlockSpec((s_len, LANES), lambda bi, i: (bi, 0))],
        out_specs=pl.BlockSpec((1, 1, nq, tq, tq), lambda bi, i: (bi, i, 0, 0, 0)),
        out_shape=jax.ShapeDtypeStruct((b, nq, nq, tq, tq), BF16),
        scratch_shapes=[pltpu.VMEM((s_len, LANES), BF16),
                        pltpu.VMEM((IDX_HEADS, tq, LANES), BF16),
                        pltpu.VMEM((s_len, tq), F32),
                        pltpu.VMEM((8, tq), I32)],
        compiler_params=_params("arbitrary", "arbitrary"),
    )(zi, kiwi, kiwi)


def _flash_body(q_ref, k_ref, v_ref, b_ref, o_ref, m_scr, l_scr, acc_scr, *, heads, nk, sub):
    i, j = pl.program_id(1), pl.program_id(2)
    tk = k_ref.shape[0]

    @pl.when(j == 0)
    def _():
        m_scr[...] = jnp.full(m_scr.shape, NEG, F32)
        l_scr[...] = jnp.zeros(l_scr.shape, F32)
        acc_scr[...] = jnp.zeros(acc_scr.shape, F32)

    @pl.when(j * sub <= i)
    def _():
        bias = jnp.concatenate([b_ref[0, 0, c] for c in range(sub)], axis=1).astype(F32)
        for h in range(heads):
            sl = slice(h * A_HEAD_DIM, (h + 1) * A_HEAD_DIM)
            s = _dot_nt(q_ref[:, sl], k_ref[:, sl]) + bias
            m_prev = m_scr[h]
            m_new = jnp.maximum(m_prev, jnp.max(s, axis=1, keepdims=True))
            alpha = jnp.exp2(m_prev - m_new)
            p = jnp.exp2(s - jnp.tile(m_new, (1, tk // LANES)))
            l_scr[h] = alpha * l_scr[h] + jnp.sum(p, axis=1, keepdims=True)
            acc_scr[:, sl] = alpha * acc_scr[:, sl] + _dot(p.astype(BF16), v_ref[:, sl])
            m_scr[h] = m_new

    @pl.when(j == nk - 1)
    def _():
        for h in range(heads):
            sl = slice(h * A_HEAD_DIM, (h + 1) * A_HEAD_DIM)
            o_ref[0, :, sl] = (acc_scr[:, sl] / l_scr[h]).astype(o_ref.dtype)


def _masked_attention(z, bias, b, s_len, bw, col, tq):
    nq = s_len // tq
    sub = min(FLASH_SUB, nq)
    nk = nq // sub
    tk = sub * tq
    heads = bw // A_HEAD_DIM
    kblk = lambda bi, i, j: bi * nk + jnp.minimum(j, i // sub)
    return pl.pallas_call(
        functools.partial(_flash_body, heads=heads, nk=nk, sub=sub),
        grid=(b, nq, nk),
        in_specs=[pl.BlockSpec((tq, bw), lambda bi, i, j: (bi * nq + i, col)),
                  pl.BlockSpec((tk, bw), lambda bi, i, j: (kblk(bi, i, j), col + 1)),
                  pl.BlockSpec((tk, bw), lambda bi, i, j: (kblk(bi, i, j), col + 2)),
                  pl.BlockSpec((1, 1, sub, tq, tq),
                               lambda bi, i, j: (bi, i, jnp.minimum(j, i // sub), 0, 0))],
        out_specs=pl.BlockSpec((1, tq, bw), lambda bi, i, j: (0, bi * nq + i, 0)),
        out_shape=jax.ShapeDtypeStruct((N_BRANCH, b * s_len, bw), BF16),
        scratch_shapes=[pltpu.VMEM((heads, tq, LANES), F32),
                        pltpu.VMEM((heads, tq, LANES), F32),
                        pltpu.VMEM((tq, bw), F32)],
        compiler_params=_params("parallel", "parallel", "arbitrary"),
    )(z, z, z, bias)


def _slab_out(slabs, slab, rows, bw, index_map):
    return dict(
        out_specs=pl.BlockSpec((1, rows, bw), lambda *g: (slab,) + tuple(index_map(*g))),
        out_shape=jax.ShapeDtypeStruct(slabs.shape, slabs.dtype),
    )


def _conformer_body(a_ref, g_ref, ah_ref, gh_ref, w_ref, bdw_ref, lng_ref, lnb_ref, slabs_ref,
                    o_ref, hbuf, ybuf, *, width):
    del slabs_ref
    i = pl.program_id(1)
    ts, bw = a_ref.shape
    ncg = bw // LANES
    h_main = a_ref[...].astype(F32) * jax.nn.sigmoid(g_ref[...].astype(F32))
    h_halo = ah_ref[...].astype(F32) * jax.nn.sigmoid(gh_ref[...].astype(F32))
    h_halo = jnp.where(i == 0, 0.0, h_halo)
    for c in range(ncg):
        sl = slice(c * LANES, (c + 1) * LANES)
        hbuf[c, 0:CONV_HALO, :] = h_halo[:, sl]
        hbuf[c, CONV_HALO:, :] = h_main[:, sl]

    def col_body(c, carry):
        wc = w_ref[c]
        acc = jnp.zeros((ts, LANES), F32)
        for k in range(width):
            off = CONV_HALO - (width - 1) + k
            acc = acc + hbuf[c, pl.ds(off, ts), :] * wc[k:k + 1, :]
        ybuf[c] = acc + bdw_ref[c]
        return carry

    lax.fori_loop(0, ncg, col_body, 0)

    tot = ybuf[0]
    for c in range(1, ncg):
        tot = tot + ybuf[c]
    mu = jnp.sum(tot, axis=1, keepdims=True) / bw
    sq = jnp.zeros((ts, LANES), F32)
    for c in range(ncg):
        d = ybuf[c] - mu
        sq = sq + d * d
    inv = lax.rsqrt(jnp.sum(sq, axis=1, keepdims=True) / bw + EPS)
    for c in range(ncg):
        y = (ybuf[c] - mu) * inv * lng_ref[c] + lnb_ref[c]
        o_ref[0, :, c * LANES:(c + 1) * LANES] = (y * jax.nn.sigmoid(y)).astype(o_ref.dtype)


def _lane_groups(v):
    return v.reshape(-1, 1, LANES)


ANY_SPEC = pl.BlockSpec(memory_space=pl.ANY)


def _conformer(z, slabs, b, s_len, bw, col, w_dw, b_dw, ln_g, ln_b, ts=256):
    width = w_dw.shape[0]
    ncg = bw // LANES
    ns = s_len // ts
    hb = ts // CONV_HALO
    w_cg = w_dw.reshape(width, ncg, LANES).transpose(1, 0, 2)
    main = lambda cb: pl.BlockSpec((ts, bw), lambda bi, i: (bi * ns + i, cb))
    halo = lambda cb: pl.BlockSpec(
        (CONV_HALO, bw), lambda bi, i: ((bi * ns + i) * hb - jnp.minimum(i, 1), cb))
    small = pl.BlockSpec((ncg, 1, LANES), lambda bi, i: (0, 0, 0))
    return pl.pallas_call(
        functools.partial(_conformer_body, width=width),
        grid=(b, ns),
        in_specs=[main(col), main(col + 1), halo(col), halo(col + 1),
                  pl.BlockSpec((ncg, width, LANES), lambda bi, i: (0, 0, 0)),
                  small, small, small, ANY_SPEC],
        input_output_aliases={8: 0},
        scratch_shapes=[pltpu.VMEM((ncg, ts + CONV_HALO, LANES), F32),
                        pltpu.VMEM((ncg, ts, LANES), F32)],
        compiler_params=_params("parallel", "parallel"),
        **_slab_out(slabs, 1, ts, bw, lambda bi, i: (bi * ns + i, 0)),
    )(z, z, z, z, w_cg, _lane_groups(b_dw), _lane_groups(ln_g), _lane_groups(ln_b), slabs)


def _gmlp_body(u_ref, v_ref, lng_ref, lnb_ref, w_ref, bias_ref, slabs_ref, o_ref):
    del slabs_ref
    ts, bw = u_ref.shape
    groups = bw // LANES
    u = jax.nn.gelu(u_ref[...].astype(F32))
    v = jax.nn.gelu(v_ref[...].astype(F32))
    mu = jnp.mean(v, axis=-1, keepdims=True)
    vc = v - mu
    var = jnp.mean(vc * vc, axis=-1, keepdims=True)
    vn = (vc * lax.rsqrt(var + EPS) * lng_ref[...] + lnb_ref[...]).astype(BF16)
    row = lax.broadcasted_iota(I32, (SG_CHUNK, SG_CHUNK), 0)
    col = lax.broadcasted_iota(I32, (SG_CHUNK, SG_CHUNK), 1)
    for g in range(groups):
        cs = slice(g * LANES, (g + 1) * LANES)
        wg = jnp.where(row >= col, w_ref[g], 0.0).astype(BF16)
        for c in range(ts // SG_CHUNK):
            rs = slice(c * SG_CHUNK, (c + 1) * SG_CHUNK)
            mixed = _dot(wg, vn[rs, cs]) + bias_ref[:, cs]
            o_ref[0, rs, cs] = (u[rs, cs] * mixed).astype(o_ref.dtype)


def _gmlp(z, slabs, m, bw, col, ln_g, ln_b, sg_w, sg_b, ts=256):
    groups = bw // LANES
    bias = jnp.broadcast_to(sg_b.T[:, :, None], (SG_CHUNK, groups, LANES)).reshape(SG_CHUNK, bw)
    row = pl.BlockSpec((1, bw), lambda i: (0, 0))
    return pl.pallas_call(
        _gmlp_body,
        grid=(m // ts,),
        in_specs=[pl.BlockSpec((ts, bw), lambda i: (i, col)),
                  pl.BlockSpec((ts, bw), lambda i: (i, col + 1)),
                  row, row,
                  pl.BlockSpec((groups, SG_CHUNK, SG_CHUNK), lambda i: (0, 0, 0)),
                  pl.BlockSpec((SG_CHUNK, bw), lambda i: (0, 0)), ANY_SPEC],
        input_output_aliases={6: 0},
        compiler_params=_params("parallel"),
        **_slab_out(slabs, 2, ts, bw, lambda i: (i, 0)),
    )(z, z, ln_g.reshape(1, bw), ln_b.reshape(1, bw), sg_w, bias, slabs)


def _shortconv_body(bg_ref, cg_ref, h_ref, cgh_ref, hh_ref, w_ref, slabs_ref, o_ref, xbuf, *, width):
    del slabs_ref
    i = pl.program_id(1)
    ts = bg_ref.shape[0]
    x_halo = cgh_ref[...].astype(F32) * hh_ref[...].astype(F32)
    xbuf[0:SC_HALO, :] = jnp.where(i == 0, 0.0, x_halo)
    xbuf[SC_HALO:, :] = cg_ref[...].astype(F32) * h_ref[...].astype(F32)
    acc = jnp.zeros(bg_ref.shape, F32)
    for k in range(width):
        off = SC_HALO - (width - 1) + k
        acc = acc + xbuf[pl.ds(off, ts), :] * w_ref[k:k + 1, :]
    o_ref[0] = (bg_ref[...].astype(F32) * acc).astype(o_ref.dtype)


def _shortconv(z, slabs, b, s_len, bw, col, w_conv, ts=256):
    width = w_conv.shape[0]
    ns = s_len // ts
    hb = ts // SC_HALO
    main = lambda cb: pl.BlockSpec((ts, bw), lambda bi, i: (bi * ns + i, cb))
    halo = lambda cb: pl.BlockSpec(
        (SC_HALO, bw), lambda bi, i: ((bi * ns + i) * hb - jnp.minimum(i, 1), cb))
    return pl.pallas_call(
        functools.partial(_shortconv_body, width=width),
        grid=(b, ns),
        in_specs=[main(col), main(col + 1), main(col + 2), halo(col + 1), halo(col + 2),
                  pl.BlockSpec((width, bw), lambda bi, i: (0, 0)), ANY_SPEC],
        input_output_aliases={6: 0},
        scratch_shapes=[pltpu.VMEM((ts + SC_HALO, bw), F32)],
        compiler_params=_params("parallel", "parallel"),
        **_slab_out(slabs, 3, ts, bw, lambda bi, i: (bi * ns + i, 0)),
    )(z, z, z, z, z, w_conv, slabs)


def _combine_body(xn_ref, wg_ref, bg_ref, o_ref, wbr_ref, y_ref, acc_ref, *, n_branch):
    n = pl.program_id(2)
    gate = jax.nn.sigmoid(_dot(xn_ref[...], wg_ref[0, 0]) + bg_ref[0, 0])
    term = gate * _dot(o_ref[0], wbr_ref[0, 0])

    @pl.when(n == 0)
    def _():
        acc_ref[...] = term

    @pl.when(jnp.logical_and(n > 0, n < n_branch - 1))
    def _():
        acc_ref[...] += term

    @pl.when(n == n_branch - 1)
    def _():
        y_ref[...] = (acc_ref[...] + term).astype(y_ref.dtype)


def _combine(xn, w_gate, b_gate, w_br, layer, o4, tm=1024, tn=512):
    m, d = xn.shape
    n_branch, bw = o4.shape[0], o4.shape[2]
    assert n_branch >= 2
    tm, tn = min(tm, m), min(tn, d)
    return pl.pallas_call(
        functools.partial(_combine_body, n_branch=n_branch),
        grid=(m // tm, d // tn, n_branch),
        in_specs=[pl.BlockSpec((tm, d), lambda i, j, n: (i, 0)),
                  pl.BlockSpec((1, 1, d, tn), lambda i, j, n: (layer, n, 0, j)),
                  pl.BlockSpec((1, 1, 1, tn), lambda i, j, n: (layer, n, 0, j)),
                  pl.BlockSpec((1, tm, bw), lambda i, j, n: (n, i, 0)),
                  pl.BlockSpec((1, 1, bw, tn), lambda i, j, n: (layer, n, 0, j))],
        out_specs=pl.BlockSpec((tm, tn), lambda i, j, n: (i, j)),
        out_shape=jax.ShapeDtypeStruct((m, d), BF16),
        scratch_shapes=[pltpu.VMEM((tm, tn), F32)],
        compiler_params=_params("parallel", "parallel", "arbitrary"),
    )(xn, w_gate, b_gate, o4, w_br)


def _xattn_body(q_ref, k_ref, v_ref, o_ref, *, heads, scale):
    hd = q_ref.shape[1] // heads
    for h in range(heads):
        sl = slice(h * hd, (h + 1) * hd)
        s = _dot_nt(q_ref[:, sl], k_ref[:, sl]) * scale
        p = jnp.exp(s - jnp.max(s, axis=1, keepdims=True))
        l = jnp.sum(p, axis=1, keepdims=True)
        o_ref[:, sl] = (_dot(p.astype(BF16), v_ref[:, sl]) / l).astype(o_ref.dtype)


def _xattn(q, k, v, b, s_len, n_mem, tq=512):
    xw = q.shape[1]
    tq = min(tq, s_len)
    nq = s_len // tq
    kv = pl.BlockSpec((n_mem, xw), lambda bi, i: (bi, 0))
    return pl.pallas_call(
        functools.partial(_xattn_body, heads=X_HEADS, scale=float((xw // X_HEADS) ** -0.5)),
        grid=(b, nq),
        in_specs=[pl.BlockSpec((tq, xw), lambda bi, i: (bi * nq + i, 0)), kv, kv],
        out_specs=pl.BlockSpec((tq, xw), lambda bi, i: (bi * nq + i, 0)),
        out_shape=jax.ShapeDtypeStruct((b * s_len, xw), BF16),
        compiler_params=_params("parallel", "parallel"),
    )(q, k, v)


def kernel(x, mem, positions, norm_mix, w_in, conv_dw_w, conv_dw_b, conv_ln_g, conv_ln_b, sg_ln_g, sg_ln_b, sg_w, sg_b, sc_w, w_gate, b_gate, w_br, w_out, norm_xattn, norm_mem, xq_w, xk_w, xv_w, xo_w, norm_mlp, mlp_w1, mlp_w2, final_norm):
    b, s_len, d = x.shape
    n_mem = mem.shape[1]
    depth = w_in.shape[0]
    n_branch = w_gate.shape[1]
    bw = w_br.shape[1] // n_branch
    m = b * s_len
    assert n_branch == N_BRANCH and bw % A_HEAD_DIM == 0 and IDX_WIDTH % bw == 0
    tq = min(256, s_len)

    tables = _rope_tables(positions)
    cos_a, sin_a, cos_i, sin_i = tables
    h = x.reshape(m, d)
    mem2 = mem.reshape(b * n_mem, d)

    o_qi = 3 * bw
    o_ki = o_qi + IDX_WIDTH
    o_wi = o_ki + IDX_DIM
    o_b = o_wi + IDX_HEADS

    w_in_bf = w_in.astype(BF16)
    w_bcd = w_in[:, :, o_b:].astype(BF16)
    w_small = jnp.concatenate(
        [w_in[:, :, o_ki:o_wi], w_in[:, :, o_ki:o_wi], w_in[:, :, o_wi:o_b],
         jnp.zeros((depth, d, SMALL_N - 2 * IDX_DIM - IDX_HEADS), F32)], axis=2).astype(BF16)
    w_gate_bf = w_gate.astype(BF16)
    b_gate4 = b_gate.reshape(depth, n_branch, 1, d)
    w_br_bf = w_br.reshape(depth, n_branch, bw, d).astype(BF16)
    w_out_bf, xq_bf, xk_bf, xv_bf, xo_bf, w1_bf, w2_bf = (
        w.astype(BF16) for w in (w_out, xq_w, xk_w, xv_w, xo_w, mlp_w1, mlp_w2))

    for l in range(depth):
        xn = _rmsnorm(h, norm_mix[l], BF16)
        za = _rope_mm(xn, w_in_bf, l, 0, 3, bw, cos_a, sin_a, BF16, period=A_HEAD_DIM,
                      half=A_ROPE_DIM // 2, n_roped=2, scale_first=True)
        zi = _rope_mm(xn, w_in_bf, l, 3, IDX_WIDTH // bw, bw, cos_i, sin_i, BF16,
                      period=IDX_DIM, half=IDX_ROPE_DIM // 2)
        kiwi = _rope_mm(xn, w_small, l, 0, 2, LANES, cos_i, sin_i, F32,
                        period=IDX_DIM, half=IDX_ROPE_DIM // 2, n_roped=1)
        zm = _matmul(xn, w_bcd, l, BF16, tn=bw)

        bias = _index_mask(zi, kiwi, b, s_len, tq)
        o4 = _masked_attention(za, bias, b, s_len, bw, 0, tq)
        o4 = _conformer(zm, o4, b, s_len, bw, 0, conv_dw_w[l], conv_dw_b[l], conv_ln_g[l], conv_ln_b[l])
        o4 = _gmlp(zm, o4, m, bw, 2, sg_ln_g[l], sg_ln_b[l], sg_w[l], sg_b[l])
        o4 = _shortconv(zm, o4, b, s_len, bw, 4, sc_w[l])

        y = _combine(xn, w_gate_bf, b_gate4, w_br_bf, l, o4)
        h = _matmul(y, w_out_bf, l, F32, res=h, tn=512)

        hn = _rmsnorm(h, norm_xattn[l], BF16)
        memn = _rmsnorm(mem2, norm_mem[l], BF16)
        qx = _matmul(hn, xq_bf, l, BF16)
        kx = _matmul(memn, xk_bf, l, BF16)
        vx = _matmul(memn, xv_bf, l, BF16)
        ox = _xattn(qx, kx, vx, b, s_len, n_mem)
        h = _matmul(ox, xo_bf, l, F32, res=h)

        hn = _rmsnorm(h, norm_mlp[l], BF16)
        a = _matmul(hn, w1_bf, l, BF16, epilogue="relu2")
        h = _matmul(a, w2_bf, l, F32, res=h, vmem_limit=VMEM_LIMIT_MAX)

    return _rmsnorm(h, final_norm, F32).reshape(b, s_len, d)
```

```python
import functools

import jax
import jax.numpy as jnp
from jax import lax
from jax.experimental import pallas as pl
from jax.experimental.pallas import tpu as pltpu

F32 = jnp.float32
BF16 = jnp.bfloat16
I32 = jnp.int32
I16 = jnp.int16

EPS = 1e-6
ROPE_THETA = 500000.0
LANES = 128
A_HEAD_DIM = 128
A_ROPE_DIM = A_HEAD_DIM // 4
IDX_HEADS = 16
IDX_DIM = 64
IDX_WIDTH = IDX_HEADS * IDX_DIM
IDX_ROPE_DIM = IDX_DIM // 4
TOPK_MAX = 256
SG_CHUNK = 128
X_HEADS = 4
NEG = -1e30
INT_MIN = -2147483648
VMEM_LIMIT = 56 * 1024 * 1024
VMEM_LIMIT_MAX = 60 * 1024 * 1024
CONV_HALO = 32
SC_HALO = 8
SMALL_N = 2 * LANES
LOG2E = 1.4426950408889634
Q_PRESCALE = float(A_HEAD_DIM ** -0.5 * LOG2E)
N_BRANCH = 4
FLASH_SUB = 4


def _params(*sem, vmem_limit=VMEM_LIMIT):
    return pltpu.CompilerParams(dimension_semantics=sem, vmem_limit_bytes=vmem_limit)


def _dot(a, b):
    return jnp.dot(a, b, preferred_element_type=F32)


def _dot_nt(a, b):
    return lax.dot_general(a, b, (((1,), (1,)), ((), ())), preferred_element_type=F32)


def _rmsnorm_body(x_ref, g_ref, o_ref):
    x = x_ref[...]
    ms = jnp.mean(x * x, axis=-1, keepdims=True)
    o_ref[...] = ((x * lax.rsqrt(ms + EPS)) * g_ref[...]).astype(o_ref.dtype)


def _rmsnorm(x, g, out_dtype):
    m, d = x.shape
    tr = min(256, m)
    return pl.pallas_call(
        _rmsnorm_body,
        grid=(m // tr,),
        in_specs=[pl.BlockSpec((tr, d), lambda i: (i, 0)),
                  pl.BlockSpec((1, d), lambda i: (0, 0))],
        out_specs=pl.BlockSpec((tr, d), lambda i: (i, 0)),
        out_shape=jax.ShapeDtypeStruct((m, d), out_dtype),
        compiler_params=_params("parallel"),
    )(x, g.reshape(1, d))


def _mm_body(*refs, nk, epilogue, has_res):
    a_ref, b_ref = refs[0], refs[1]
    res_ref = refs[2] if has_res else None
    o_ref = refs[-1]

    part = _dot(a_ref[...], b_ref[0])
    if nk == 1:
        if epilogue == "relu2":
            r = jnp.maximum(part, 0.0)
            part = r * r
        if has_res:
            part = res_ref[...] + part
        o_ref[...] = part.astype(o_ref.dtype)
        return

    k = pl.program_id(2)

    @pl.when(k == 0)
    def _():
        o_ref[...] = res_ref[...] + part if has_res else part

    @pl.when(k > 0)
    def _():
        o_ref[...] += part


def _matmul(a, w, layer, out_dtype, res=None, epilogue=None, tm=1024, tn=1024, tk=4096,
            vmem_limit=VMEM_LIMIT):
    m, kdim = a.shape
    n = w.shape[2]
    tm, tn, tk = min(tm, m), min(tn, n), min(tk, kdim)
    nk = kdim // tk
    assert nk == 1 or (out_dtype == F32 and epilogue is None)
    in_specs = [pl.BlockSpec((tm, tk), lambda i, j, k: (i, k)),
                pl.BlockSpec((1, tk, tn), lambda i, j, k: (layer, k, j))]
    args = [a, w]
    if res is not None:
        in_specs.append(pl.BlockSpec((tm, tn), lambda i, j, k: (i, j)))
        args.append(res)
    return pl.pallas_call(
        functools.partial(_mm_body, nk=nk, epilogue=epilogue, has_res=res is not None),
        grid=(m // tm, n // tn, nk),
        in_specs=in_specs,
        out_specs=pl.BlockSpec((tm, tn), lambda i, j, k: (i, j)),
        out_shape=jax.ShapeDtypeStruct((m, n), out_dtype),
        compiler_params=_params("parallel", "parallel", "arbitrary", vmem_limit=vmem_limit),
    )(*args)


def _tables_body(pos_ref, inva_ref, invi_ref, ca_ref, sa_ref, ci_ref, si_ref):
    p = pos_ref[...]
    ang_a = p * inva_ref[...]
    ang_i = p * invi_ref[...]
    ca_ref[...] = jnp.cos(ang_a)
    sa_ref[...] = jnp.sin(ang_a)
    ci_ref[...] = jnp.cos(ang_i)
    si_ref[...] = jnp.sin(ang_i)


def _rope_tables(positions):
    b, s = positions.shape
    m = b * s
    pos = jnp.broadcast_to(positions.astype(F32).reshape(m, 1), (m, LANES))

    def inv_row(rot, period):
        inv = ROPE_THETA ** (-jnp.arange(0, rot, 2, dtype=F32) / rot)
        one = jnp.concatenate([inv, inv, jnp.zeros((period - rot,), F32)])
        return jnp.tile(one, LANES // period).reshape(1, LANES)

    tr = min(1024, m)
    spec = pl.BlockSpec((tr, LANES), lambda i: (i, 0))
    row = pl.BlockSpec((1, LANES), lambda i: (0, 0))
    sds = jax.ShapeDtypeStruct((m, LANES), F32)
    return pl.pallas_call(
        _tables_body,
        grid=(m // tr,),
        in_specs=[spec, row, row],
        out_specs=[spec] * 4,
        out_shape=[sds] * 4,
        compiler_params=_params("parallel"),
    )(pos, inv_row(A_ROPE_DIM, A_HEAD_DIM), inv_row(IDX_ROPE_DIM, IDX_DIM))


def _rope_group(x, cos, sin, first_half, half):
    rot = jnp.where(first_half, -pltpu.roll(x, LANES - half, 1), pltpu.roll(x, half, 1))
    return x * cos + sin * rot


def _first_half_mask(rows, period, half):
    lane = lax.broadcasted_iota(I32, (rows, LANES), 1)
    return (lane & (period - 1)) < half


def _rope_mm_body(a_ref, b_ref, cos_ref, sin_ref, o_ref, *, period, half, n_roped, scale_first):
    j = pl.program_id(1)
    acc = _dot(a_ref[...], b_ref[0])
    tm, tn = acc.shape
    first = _first_half_mask(tm, period, half)
    cos, sin = cos_ref[...], sin_ref[...]
    if n_roped is not None:
        roped = j < n_roped
        cos = jnp.where(roped, cos, 1.0)
        sin = jnp.where(roped, sin, 0.0)
    if scale_first:
        cos = cos * jnp.where(j == 0, Q_PRESCALE, 1.0)
        sin = sin * jnp.where(j == 0, Q_PRESCALE, 1.0)
    for g in range(tn // LANES):
        sl = slice(g * LANES, (g + 1) * LANES)
        o_ref[:, sl] = _rope_group(acc[:, sl], cos, sin, first, half).astype(o_ref.dtype)


def _rope_mm(xn, w, layer, col0, n_blocks, tn, cos, sin, out_dtype, *, period, half,
             n_roped=None, scale_first=False):
    m, d = xn.shape
    tm = min(1024, m)
    tab = pl.BlockSpec((tm, LANES), lambda i, j: (i, 0))
    return pl.pallas_call(
        functools.partial(_rope_mm_body, period=period, half=half, n_roped=n_roped,
                          scale_first=scale_first),
        grid=(m // tm, n_blocks),
        in_specs=[pl.BlockSpec((tm, d), lambda i, j: (i, 0)),
                  pl.BlockSpec((1, d, tn), lambda i, j: (layer, 0, col0 + j)),
                  tab, tab],
        out_specs=pl.BlockSpec((tm, tn), lambda i, j: (i, j)),
        out_shape=jax.ShapeDtypeStruct((m, n_blocks * tn), out_dtype),
        compiler_params=_params("parallel", "parallel"),
    )(xn, w, cos, sin)


def _sortable(score):
    bits = lax.bitcast_convert_type(score, I32)
    return jnp.where(bits < 0, bits ^ jnp.int32(0x7FFFFFFF), bits)


def _index_body(qi_ref, w_ref, ki_ref, bias_ref, kbf_scr, qm_scr, sc_scr, pk_scr, jthr_scr,
                *, tq, topk, scale, s_len):
    i = pl.program_id(1)
    n_chunks = s_len // tq
    n_live = i + 1

    @pl.when(i == 0)
    def _():
        kbf_scr[...] = ki_ref[...].astype(BF16)

    lane = lax.broadcasted_iota(I32, (tq, LANES), 1)
    for p in range(IDX_HEADS // 2):
        qp = qi_ref[:, p * LANES:(p + 1) * LANES]
        zero = jnp.zeros_like(qp)
        qm_scr[2 * p] = jnp.where(lane < IDX_DIM, qp, zero)
        qm_scr[2 * p + 1] = jnp.where(lane >= IDX_DIM, qp, zero)
    w_t = w_ref[...].T

    def chunk_rows(c):
        return pl.ds(pl.multiple_of(c * tq, tq), tq)

    def score_body(c, carry):
        rows = chunk_rows(c)
        kc = kbf_scr[rows, :]
        for h in range(0, IDX_HEADS, 2):
            rel = (jnp.maximum(_dot_nt(kc, qm_scr[h]), 0.0) * w_t[h:h + 1, :]
                   + jnp.maximum(_dot_nt(kc, qm_scr[h + 1]), 0.0) * w_t[h + 1:h + 2, :])
            if h == 0:
                sc_scr[rows, :] = rel
            else:
                sc_scr[rows, :] += rel
        return carry

    lax.fori_loop(0, n_live, score_body, 0)

    q_pos = i * tq + lax.broadcasted_iota(I32, (tq, tq), 1)
    k_off = lax.broadcasted_iota(I32, (tq, tq), 0)

    def key_body(c, carry):
        rows = chunk_rows(c)
        key = _sortable(sc_scr[rows, :] * scale)
        key = jnp.where(c * tq + k_off <= q_pos, key, jnp.int32(INT_MIN))
        sc_scr[rows, :] = lax.bitcast_convert_type(key, F32)
        return carry

    lax.fori_loop(0, n_live, key_body, 0)

    def load_keys(c):
        return lax.bitcast_convert_type(sc_scr[chunk_rows(c), :], I32)

    def count(pred):
        def hits(c):
            hit = pred(load_keys(c), c * tq + k_off).astype(I32)
            return hit.reshape(tq // 8, 8, tq).sum(axis=0)

        def pair_body(g, cnt8):
            return cnt8 + hits(2 * g) + hits(2 * g + 1)

        cnt8 = lax.fori_loop(0, jnp.right_shift(n_live, 1), pair_body, jnp.zeros((8, tq), I32))
        cnt8 = cnt8 + jnp.where((n_live & 1) == 1, hits(n_live - 1), 0)
        return cnt8.sum(axis=0, keepdims=True)

    full = q_pos[0:1, :] + 1 >= topk
    unknown = jnp.int32(s_len)

    def count16(pred):
        def hits(c):
            hit = jnp.where(pred(pk_scr[chunk_rows(c), :]), jnp.int16(1), jnp.int16(0))
            parts = hit.reshape(tq // 16, 16, tq)
            total = parts[0]
            for r in range(1, tq // 16):
                total = total + parts[r]
            return total

        def pair_body(g, cnt):
            return cnt + hits(2 * g) + hits(2 * g + 1)

        cnt = lax.fori_loop(0, jnp.right_shift(n_live, 1), pair_body, jnp.zeros((16, tq), I16))
        cnt = cnt + jnp.where((n_live & 1) == 1, hits(n_live - 1), jnp.int16(0))
        return cnt.astype(I32).sum(axis=0, keepdims=True)

    def bisect16(target, n_open):
        cnt0 = count16(lambda v: v >= 0)
        nonneg = cnt0 >= target
        prefix0 = jnp.where(nonneg, jnp.int32(0), jnp.int32(-32768))
        cntp0 = jnp.where(nonneg, cnt0, unknown)

        def cond(carry):
            it, _, _, still_open = carry
            return jnp.logical_and(it < 15, still_open > 0)

        def body(carry):
            it, prefix, cntp, _ = carry
            cand = prefix | jnp.left_shift(jnp.int32(1), jnp.int32(14) - it)
            cand16 = cand.astype(I16)
            cnt = count16(lambda v: v >= cand16)
            take = cnt >= target
            cntp = jnp.where(take, cnt, cntp)
            return it + 1, jnp.where(take, cand, prefix), cntp, n_open(cntp)

        _, prefix, cntp, _ = lax.while_loop(cond, body, (jnp.int32(0), prefix0, cntp0, n_open(cntp0)))
        return prefix, cntp

    def open_lanes(cnt_sel):
        return jnp.max(jnp.where(jnp.logical_and(full, cnt_sel != topk), 1, 0))

    def pack_hi_body(c, carry):
        pk_scr[chunk_rows(c), :] = jnp.right_shift(load_keys(c), 16).astype(I16)
        return carry

    lax.fori_loop(0, n_live, pack_hi_body, 0)
    thr_hi, _ = bisect16(topk, open_lanes)
    thr_hi16 = thr_hi.astype(I16)
    cnt_above = count16(lambda v: v > thr_hi16)

    def pack_lo_body(c, carry):
        key = load_keys(c)
        lo = (key & 0xFFFF) - 32768
        lo = jnp.where(jnp.right_shift(key, 16) == thr_hi, lo, -32768)
        pk_scr[chunk_rows(c), :] = lo.astype(I16)
        return carry

    lax.fori_loop(0, n_live, pack_lo_body, 0)
    thr_lo, cnt_lo = bisect16(topk - cnt_above, lambda cntp: open_lanes(cnt_above + cntp))
    thr = thr_hi * 65536 + (thr_lo + 32768)
    cnt_ge = cnt_above + cnt_lo

    tie = jnp.logical_and(cnt_ge > topk, thr != INT_MIN)
    jthr_scr[...] = jnp.full((8, tq), s_len, I32)

    @pl.when(jnp.max(tie.astype(I32)) > 0)
    def _():
        n_bits = max(1, (s_len - 1).bit_length())
        need = topk - count(lambda key, idx: key > thr)

        def idx_body(it, ans):
            cand = ans | jnp.left_shift(jnp.int32(1), jnp.int32(n_bits - 1) - it)
            cnt = count(lambda key, idx: jnp.logical_and(key == thr, idx < cand))
            return jnp.where(cnt < need, cand, ans)

        ans = lax.fori_loop(0, n_bits, idx_body, jnp.zeros((1, tq), I32))
        jthr_scr[...] = jnp.broadcast_to(jnp.where(tie, ans, s_len), (8, tq))

    jthr = jthr_scr[0:1, :]

    def out_body(c, carry):
        key = load_keys(c)
        idx = c * tq + k_off
        sel = jnp.logical_or(key > thr, jnp.logical_and(key == thr, idx <= jthr))
        sel = jnp.logical_and(sel, idx <= q_pos)
        bias_t = jnp.where(sel, 0.0, NEG)
        bias_ref[0, 0, c] = bias_t.T.astype(bias_ref.dtype)
        return carry

    lax.fori_loop(0, n_live, out_body, 0)

    def fill_body(c, carry):
        bias_ref[0, 0, c] = jnp.full((tq, tq), NEG, bias_ref.dtype)
        return carry

    lax.fori_loop(n_live, n_chunks, fill_body, 0)


def _index_mask(zi, kiwi, b, s_len, tq):
    nq = s_len // tq
    topk = min(TOPK_MAX, s_len // 4)
    return pl.pallas_call(
        functools.partial(_index_body, tq=tq, topk=topk, s_len=s_len,
                          scale=float((IDX_DIM * IDX_HEADS) ** -0.5)),
        grid=(b, nq),
        in_specs=[pl.BlockSpec((tq, IDX_WIDTH), lambda bi, i: (bi * nq + i, 0)),
                  pl.BlockSpec((tq, LANES), lambda bi, i: (bi * nq + i, 1)),
                  pl.BlockSpec((s_len, LANES), lambda bi, i: (bi, 0))],
        out_specs=pl.BlockSpec((1, 1, nq, tq, tq), lambda bi, i: (bi, i, 0, 0, 0)),
        out_shape=jax.ShapeDtypeStruct((b, nq, nq, tq, tq), BF16),
        scratch_shapes=[pltpu.VMEM((s_len, LANES), BF16),
                        pltpu.VMEM((IDX_HEADS, tq, LANES), BF16),
                        pltpu.VMEM((s_len, tq), F32),
                        pltpu.VMEM((s_len, tq), I16),
                        pltpu.VMEM((8, tq), I32)],
        compiler_params=_params("arbitrary", "arbitrary"),
    )(zi, kiwi, kiwi)


def _flash_body(q_ref, k_ref, v_ref, b_ref, o_ref, m_scr, l_scr, acc_scr, *, heads, nk, sub):
    i, j = pl.program_id(1), pl.program_id(2)
    tk = k_ref.shape[0]

    @pl.when(j == 0)
    def _():
        m_scr[...] = jnp.full(m_scr.shape, NEG, F32)
        l_scr[...] = jnp.zeros(l_scr.shape, F32)
        acc_scr[...] = jnp.zeros(acc_scr.shape, F32)

    @pl.when(j * sub <= i)
    def _():
        bias = jnp.concatenate([b_ref[0, 0, c] for c in range(sub)], axis=1).astype(F32)
        for h in range(heads):
            sl = slice(h * A_HEAD_DIM, (h + 1) * A_HEAD_DIM)
            s = _dot_nt(q_ref[:, sl], k_ref[:, sl]) + bias
            m_prev = m_scr[h]
            m_new = jnp.maximum(m_prev, jnp.max(s, axis=1, keepdims=True))
            alpha = jnp.exp2(m_prev - m_new)
            p = jnp.exp2(s - jnp.tile(m_new, (1, tk // LANES)))
            l_scr[h] = alpha * l_scr[h] + jnp.sum(p, axis=1, keepdims=True)
            acc_scr[:, sl] = alpha * acc_scr[:, sl] + _dot(p.astype(BF16), v_ref[:, sl])
            m_scr[h] = m_new

    @pl.when(j == nk - 1)
    def _():
        for h in range(heads):
            sl = slice(h * A_HEAD_DIM, (h + 1) * A_HEAD_DIM)
            o_ref[0, :, sl] = (acc_scr[:, sl] / l_scr[h]).astype(o_ref.dtype)


def _masked_attention(z, bias, b, s_len, bw, col, tq):
    nq = s_len // tq
    sub = min(FLASH_SUB, nq)
    nk = nq // sub
    tk = sub * tq
    heads = bw // A_HEAD_DIM
    kblk = lambda bi, i, j: bi * nk + jnp.minimum(j, i // sub)
    return pl.pallas_call(
        functools.partial(_flash_body, heads=heads, nk=nk, sub=sub),
        grid=(b, nq, nk),
        in_specs=[pl.BlockSpec((tq, bw), lambda bi, i, j: (bi * nq + i, col)),
                  pl.BlockSpec((tk, bw), lambda bi, i, j: (kblk(bi, i, j), col + 1)),
                  pl.BlockSpec((tk, bw), lambda bi, i, j: (kblk(bi, i, j), col + 2)),
                  pl.BlockSpec((1, 1, sub, tq, tq),
                               lambda bi, i, j: (bi, i, jnp.minimum(j, i // sub), 0, 0))],
        out_specs=pl.BlockSpec((1, tq, bw), lambda bi, i, j: (0, bi * nq + i, 0)),
        out_shape=jax.ShapeDtypeStruct((N_BRANCH, b * s_len, bw), BF16),
        scratch_shapes=[pltpu.VMEM((heads, tq, LANES), F32),
                        pltpu.VMEM((heads, tq, LANES), F32),
                        pltpu.VMEM((tq, bw), F32)],
        compiler_params=_params("parallel", "parallel", "arbitrary"),
    )(z, z, z, bias)


def _slab_out(slabs, slab, rows, bw, index_map):
    return dict(
        out_specs=pl.BlockSpec((1, rows, bw), lambda *g: (slab,) + tuple(index_map(*g))),
        out_shape=jax.ShapeDtypeStruct(slabs.shape, slabs.dtype),
    )


def _conformer_body(a_ref, g_ref, ah_ref, gh_ref, w_ref, bdw_ref, lng_ref, lnb_ref, slabs_ref,
                    o_ref, hbuf, ybuf, *, width):
    del slabs_ref
    i = pl.program_id(1)
    ts, bw = a_ref.shape
    ncg = bw // LANES
    h_main = a_ref[...].astype(F32) * jax.nn.sigmoid(g_ref[...].astype(F32))
    h_halo = ah_ref[...].astype(F32) * jax.nn.sigmoid(gh_ref[...].astype(F32))
    h_halo = jnp.where(i == 0, 0.0, h_halo)
    for c in range(ncg):
        sl = slice(c * LANES, (c + 1) * LANES)
        hbuf[c, 0:CONV_HALO, :] = h_halo[:, sl]
        hbuf[c, CONV_HALO:, :] = h_main[:, sl]

    def col_body(c, carry):
        wc = w_ref[c]
        acc = jnp.zeros((ts, LANES), F32)
        for k in range(width):
            off = CONV_HALO - (width - 1) + k
            acc = acc + hbuf[c, pl.ds(off, ts), :] * wc[k:k + 1, :]
        ybuf[c] = acc + bdw_ref[c]
        return carry

    lax.fori_loop(0, ncg, col_body, 0)

    tot = ybuf[0]
    for c in range(1, ncg):
        tot = tot + ybuf[c]
    mu = jnp.sum(tot, axis=1, keepdims=True) / bw
    sq = jnp.zeros((ts, LANES), F32)
    for c in range(ncg):
        d = ybuf[c] - mu
        sq = sq + d * d
    inv = lax.rsqrt(jnp.sum(sq, axis=1, keepdims=True) / bw + EPS)
    for c in range(ncg):
        y = (ybuf[c] - mu) * inv * lng_ref[c] + lnb_ref[c]
        o_ref[0, :, c * LANES:(c + 1) * LANES] = (y * jax.nn.sigmoid(y)).astype(o_ref.dtype)


def _lane_groups(v):
    return v.reshape(-1, 1, LANES)


ANY_SPEC = pl.BlockSpec(memory_space=pl.ANY)


def _conformer(z, slabs, b, s_len, bw, col, w_dw, b_dw, ln_g, ln_b, ts=256):
    width = w_dw.shape[0]
    ncg = bw // LANES
    ns = s_len // ts
    hb = ts // CONV_HALO
    w_cg = w_dw.reshape(width, ncg, LANES).transpose(1, 0, 2)
    main = lambda cb: pl.BlockSpec((ts, bw), lambda bi, i: (bi * ns + i, cb))
    halo = lambda cb: pl.BlockSpec(
        (CONV_HALO, bw), lambda bi, i: ((bi * ns + i) * hb - jnp.minimum(i, 1), cb))
    small = pl.BlockSpec((ncg, 1, LANES), lambda bi, i: (0, 0, 0))
    return pl.pallas_call(
        functools.partial(_conformer_body, width=width),
        grid=(b, ns),
        in_specs=[main(col), main(col + 1), halo(col), halo(col + 1),
                  pl.BlockSpec((ncg, width, LANES), lambda bi, i: (0, 0, 0)),
                  small, small, small, ANY_SPEC],
        input_output_aliases={8: 0},
        scratch_shapes=[pltpu.VMEM((ncg, ts + CONV_HALO, LANES), F32),
                        pltpu.VMEM((ncg, ts, LANES), F32)],
        compiler_params=_params("parallel", "parallel"),
        **_slab_out(slabs, 1, ts, bw, lambda bi, i: (bi * ns + i, 0)),
    )(z, z, z, z, w_cg, _lane_groups(b_dw), _lane_groups(ln_g), _lane_groups(ln_b), slabs)


def _gmlp_body(u_ref, v_ref, lng_ref, lnb_ref, w_ref, bias_ref, slabs_ref, o_ref):
    del slabs_ref
    ts, bw = u_ref.shape
    groups = bw // LANES
    u = jax.nn.gelu(u_ref[...].astype(F32))
    v = jax.nn.gelu(v_ref[...].astype(F32))
    mu = jnp.mean(v, axis=-1, keepdims=True)
    vc = v - mu
    var = jnp.mean(vc * vc, axis=-1, keepdims=True)
    vn = (vc * lax.rsqrt(var + EPS) * lng_ref[...] + lnb_ref[...]).astype(BF16)
    row = lax.broadcasted_iota(I32, (SG_CHUNK, SG_CHUNK), 0)
    col = lax.broadcasted_iota(I32, (SG_CHUNK, SG_CHUNK), 1)
    for g in range(groups):
        cs = slice(g * LANES, (g + 1) * LANES)
        wg = jnp.where(row >= col, w_ref[g], 0.0).astype(BF16)
        for c in range(ts // SG_CHUNK):
            rs = slice(c * SG_CHUNK, (c + 1) * SG_CHUNK)
            mixed = _dot(wg, vn[rs, cs]) + bias_ref[:, cs]
            o_ref[0, rs, cs] = (u[rs, cs] * mixed).astype(o_ref.dtype)


def _gmlp(z, slabs, m, bw, col, ln_g, ln_b, sg_w, sg_b, ts=256):
    groups = bw // LANES
    bias = jnp.broadcast_to(sg_b.T[:, :, None], (SG_CHUNK, groups, LANES)).reshape(SG_CHUNK, bw)
    row = pl.BlockSpec((1, bw), lambda i: (0, 0))
    return pl.pallas_call(
        _gmlp_body,
        grid=(m // ts,),
        in_specs=[pl.BlockSpec((ts, bw), lambda i: (i, col)),
                  pl.BlockSpec((ts, bw), lambda i: (i, col + 1)),
                  row, row,
                  pl.BlockSpec((groups, SG_CHUNK, SG_CHUNK), lambda i: (0, 0, 0)),
                  pl.BlockSpec((SG_CHUNK, bw), lambda i: (0, 0)), ANY_SPEC],
        input_output_aliases={6: 0},
        compiler_params=_params("parallel"),
        **_slab_out(slabs, 2, ts, bw, lambda i: (i, 0)),
    )(z, z, ln_g.reshape(1, bw), ln_b.reshape(1, bw), sg_w, bias, slabs)


def _shortconv_body(bg_ref, cg_ref, h_ref, cgh_ref, hh_ref, w_ref, slabs_ref, o_ref, xbuf, *, width):
    del slabs_ref
    i = pl.program_id(1)
    ts = bg_ref.shape[0]
    x_halo = cgh_ref[...].astype(F32) * hh_ref[...].astype(F32)
    xbuf[0:SC_HALO, :] = jnp.where(i == 0, 0.0, x_halo)
    xbuf[SC_HALO:, :] = cg_ref[...].astype(F32) * h_ref[...].astype(F32)
    acc = jnp.zeros(bg_ref.shape, F32)
    for k in range(width):
        off = SC_HALO - (width - 1) + k
        acc = acc + xbuf[pl.ds(off, ts), :] * w_ref[k:k + 1, :]
    o_ref[0] = (bg_ref[...].astype(F32) * acc).astype(o_ref.dtype)


def _shortconv(z, slabs, b, s_len, bw, col, w_conv, ts=256):
    width = w_conv.shape[0]
    ns = s_len // ts
    hb = ts // SC_HALO
    main = lambda cb: pl.BlockSpec((ts, bw), lambda bi, i: (bi * ns + i, cb))
    halo = lambda cb: pl.BlockSpec(
        (SC_HALO, bw), lambda bi, i: ((bi * ns + i) * hb - jnp.minimum(i, 1), cb))
    return pl.pallas_call(
        functools.partial(_shortconv_body, width=width),
        grid=(b, ns),
        in_specs=[main(col), main(col + 1), main(col + 2), halo(col + 1), halo(col + 2),
                  pl.BlockSpec((width, bw), lambda bi, i: (0, 0)), ANY_SPEC],
        input_output_aliases={6: 0},
        scratch_shapes=[pltpu.VMEM((ts + SC_HALO, bw), F32)],
        compiler_params=_params("parallel", "parallel"),
        **_slab_out(slabs, 3, ts, bw, lambda bi, i: (bi * ns + i, 0)),
    )(z, z, z, z, z, w_conv, slabs)


def _combine_body(xn_ref, wg_ref, bg_ref, o_ref, wbr_ref, y_ref, acc_ref, *, n_branch):
    n = pl.program_id(2)
    gate = jax.nn.sigmoid(_dot(xn_ref[...], wg_ref[0, 0]) + bg_ref[0, 0])
    term = gate * _dot(o_ref[0], wbr_ref[0, 0])

    @pl.when(n == 0)
    def _():
        acc_ref[...] = term

    @pl.when(jnp.logical_and(n > 0, n < n_branch - 1))
    def _():
        acc_ref[...] += term

    @pl.when(n == n_branch - 1)
    def _():
        y_ref[...] = (acc_ref[...] + term).astype(y_ref.dtype)


def _combine(xn, w_gate, b_gate, w_br, layer, o4, tm=1024, tn=512):
    m, d = xn.shape
    n_branch, bw = o4.shape[0], o4.shape[2]
    assert n_branch >= 2
    tm, tn = min(tm, m), min(tn, d)
    return pl.pallas_call(
        functools.partial(_combine_body, n_branch=n_branch),
        grid=(m // tm, d // tn, n_branch),
        in_specs=[pl.BlockSpec((tm, d), lambda i, j, n: (i, 0)),
                  pl.BlockSpec((1, 1, d, tn), lambda i, j, n: (layer, n, 0, j)),
                  pl.BlockSpec((1, 1, 1, tn), lambda i, j, n: (layer, n, 0, j)),
                  pl.BlockSpec((1, tm, bw), lambda i, j, n: (n, i, 0)),
                  pl.BlockSpec((1, 1, bw, tn), lambda i, j, n: (layer, n, 0, j))],
        out_specs=pl.BlockSpec((tm, tn), lambda i, j, n: (i, j)),
        out_shape=jax.ShapeDtypeStruct((m, d), BF16),
        scratch_shapes=[pltpu.VMEM((tm, tn), F32)],
        compiler_params=_params("parallel", "parallel", "arbitrary"),
    )(xn, w_gate, b_gate, o4, w_br)


def _xattn_body(q_ref, k_ref, v_ref, o_ref, *, heads, scale):
    hd = q_ref.shape[1] // heads
    for h in range(heads):
        sl = slice(h * hd, (h + 1) * hd)
        s = _dot_nt(q_ref[:, sl], k_ref[:, sl]) * scale
        p = jnp.exp(s - jnp.max(s, axis=1, keepdims=True))
        l = jnp.sum(p, axis=1, keepdims=True)
        o_ref[:, sl] = (_dot(p.astype(BF16), v_ref[:, sl]) / l).astype(o_ref.dtype)


def _xattn(q, k, v, b, s_len, n_mem, tq=512):
    xw = q.shape[1]
    tq = min(tq, s_len)
    nq = s_len // tq
    kv = pl.BlockSpec((n_mem, xw), lambda bi, i: (bi, 0))
    return pl.pallas_call(
        functools.partial(_xattn_body, heads=X_HEADS, scale=float((xw // X_HEADS) ** -0.5)),
        grid=(b, nq),
        in_specs=[pl.BlockSpec((tq, xw), lambda bi, i: (bi * nq + i, 0)), kv, kv],
        out_specs=pl.BlockSpec((tq, xw), lambda bi, i: (bi * nq + i, 0)),
        out_shape=jax.ShapeDtypeStruct((b * s_len, xw), BF16),
        compiler_params=_params("parallel", "parallel"),
    )(q, k, v)


def kernel(x, mem, positions, norm_mix, w_in, conv_dw_w, conv_dw_b, conv_ln_g, conv_ln_b, sg_ln_g, sg_ln_b, sg_w, sg_b, sc_w, w_gate, b_gate, w_br, w_out, norm_xattn, norm_mem, xq_w, xk_w, xv_w, xo_w, norm_mlp, mlp_w1, mlp_w2, final_norm):
    b, s_len, d = x.shape
    n_mem = mem.shape[1]
    depth = w_in.shape[0]
    n_branch = w_gate.shape[1]
    bw = w_br.shape[1] // n_branch
    m = b * s_len
    assert n_branch == N_BRANCH and bw % A_HEAD_DIM == 0 and IDX_WIDTH % bw == 0
    tq = min(256, s_len)

    tables = _rope_tables(positions)
    cos_a, sin_a, cos_i, sin_i = tables
    h = x.reshape(m, d)
    mem2 = mem.reshape(b * n_mem, d)

    o_qi = 3 * bw
    o_ki = o_qi + IDX_WIDTH
    o_wi = o_ki + IDX_DIM
    o_b = o_wi + IDX_HEADS

    w_in_bf = w_in[:, :, :o_ki].astype(BF16)
    w_bcd = w_in[:, :, o_b:].astype(BF16)
    w_small = jnp.concatenate(
        [w_in[:, :, o_ki:o_wi], w_in[:, :, o_ki:o_wi], w_in[:, :, o_wi:o_b],
         jnp.zeros((depth, d, SMALL_N - 2 * IDX_DIM - IDX_HEADS), F32)], axis=2).astype(BF16)
    w_gate_bf = w_gate.astype(BF16)
    b_gate4 = b_gate.reshape(depth, n_branch, 1, d)
    w_br_bf = w_br.reshape(depth, n_branch, bw, d).astype(BF16)
    w_out_bf, xq_bf, xk_bf, xv_bf, xo_bf, w1_bf, w2_bf = (
        w.astype(BF16) for w in (w_out, xq_w, xk_w, xv_w, xo_w, mlp_w1, mlp_w2))

    for l in range(depth):
        xn = _rmsnorm(h, norm_mix[l], BF16)
        za = _rope_mm(xn, w_in_bf, l, 0, 3, bw, cos_a, sin_a, BF16, period=A_HEAD_DIM,
                      half=A_ROPE_DIM // 2, n_roped=2, scale_first=True)
        zi = _rope_mm(xn, w_in_bf, l, 3, IDX_WIDTH // bw, bw, cos_i, sin_i, BF16,
                      period=IDX_DIM, half=IDX_ROPE_DIM // 2)
        kiwi = _rope_mm(xn, w_small, l, 0, 2, LANES, cos_i, sin_i, F32,
                        period=IDX_DIM, half=IDX_ROPE_DIM // 2, n_roped=1)
        zm = _matmul(xn, w_bcd, l, BF16, tn=bw)

        bias = _index_mask(zi, kiwi, b, s_len, tq)
        o4 = _masked_attention(za, bias, b, s_len, bw, 0, tq)
        o4 = _conformer(zm, o4, b, s_len, bw, 0, conv_dw_w[l], conv_dw_b[l], conv_ln_g[l], conv_ln_b[l])
        o4 = _gmlp(zm, o4, m, bw, 2, sg_ln_g[l], sg_ln_b[l], sg_w[l], sg_b[l])
        o4 = _shortconv(zm, o4, b, s_len, bw, 4, sc_w[l])

        y = _combine(xn, w_gate_bf, b_gate4, w_br_bf, l, o4)
        h = _matmul(y, w_out_bf, l, F32, res=h, tn=512)

        hn = _rmsnorm(h, norm_xattn[l], BF16)
        memn = _rmsnorm(mem2, norm_mem[l], BF16)
        qx = _matmul(hn, xq_bf, l, BF16)
        kx = _matmul(memn, xk_bf, l, BF16)
        vx = _matmul(memn, xv_bf, l, BF16)
        ox = _xattn(qx, kx, vx, b, s_len, n_mem)
        h = _matmul(ox, xo_bf, l, F32, res=h)

        hn = _rmsnorm(h, norm_mlp[l], BF16)
        a = _matmul(hn, w1_bf, l, BF16, epilogue="relu2")
        h = _matmul(a, w2_bf, l, F32, res=h, vmem_limit=VMEM_LIMIT_MAX)

    return _rmsnorm(h, final_norm, F32).reshape(b, s_len, d)
```

```python
import functools

import jax
import jax.numpy as jnp
from jax import lax
from jax.experimental import pallas as pl
from jax.experimental.pallas import tpu as pltpu

F32 = jnp.float32
BF16 = jnp.bfloat16
I32 = jnp.int32
I16 = jnp.int16

EPS = 1e-6
ROPE_THETA = 500000.0
LANES = 128
A_HEAD_DIM = 128
A_ROPE_DIM = A_HEAD_DIM // 4
IDX_HEADS = 16
IDX_DIM = 64
IDX_WIDTH = IDX_HEADS * IDX_DIM
IDX_ROPE_DIM = IDX_DIM // 4
TOPK_MAX = 256
SG_CHUNK = 128
X_HEADS = 4
NEG = -1e30
INT_MIN = -2147483648
VMEM_LIMIT = 56 * 1024 * 1024
VMEM_LIMIT_MAX = 60 * 1024 * 1024
CONV_HALO = 32
SC_HALO = 8
SMALL_N = 2 * LANES
LOG2E = 1.4426950408889634
Q_PRESCALE = float(A_HEAD_DIM ** -0.5 * LOG2E)
N_BRANCH = 4
FLASH_SUB = 4


def _params(*sem, vmem_limit=VMEM_LIMIT):
    return pltpu.CompilerParams(dimension_semantics=sem, vmem_limit_bytes=vmem_limit)


def _dot(a, b):
    return jnp.dot(a, b, preferred_element_type=F32)


def _dot_nt(a, b):
    return lax.dot_general(a, b, (((1,), (1,)), ((), ())), preferred_element_type=F32)


def _rmsnorm_body(x_ref, g_ref, o_ref):
    x = x_ref[...]
    ms = jnp.mean(x * x, axis=-1, keepdims=True)
    o_ref[...] = ((x * lax.rsqrt(ms + EPS)) * g_ref[...]).astype(o_ref.dtype)


def _rmsnorm(x, g, out_dtype):
    m, d = x.shape
    tr = min(512, m)
    return pl.pallas_call(
        _rmsnorm_body,
        grid=(m // tr,),
        in_specs=[pl.BlockSpec((tr, d), lambda i: (i, 0)),
                  pl.BlockSpec((1, d), lambda i: (0, 0))],
        out_specs=pl.BlockSpec((tr, d), lambda i: (i, 0)),
        out_shape=jax.ShapeDtypeStruct((m, d), out_dtype),
        compiler_params=_params("parallel"),
    )(x, g.reshape(1, d))


def _mm_body(*refs, nk, epilogue, has_res):
    a_ref, b_ref = refs[0], refs[1]
    res_ref = refs[2] if has_res else None
    o_ref = refs[-1]

    part = _dot(a_ref[...], b_ref[0])
    if nk == 1:
        if epilogue == "relu2":
            r = jnp.maximum(part, 0.0)
            part = r * r
        if has_res:
            part = res_ref[...] + part
        o_ref[...] = part.astype(o_ref.dtype)
        return

    k = pl.program_id(2)

    @pl.when(k == 0)
    def _():
        o_ref[...] = res_ref[...] + part if has_res else part

    @pl.when(k > 0)
    def _():
        o_ref[...] += part


def _matmul(a, w, layer, out_dtype, res=None, epilogue=None, tm=1024, tn=1024, tk=4096,
            vmem_limit=VMEM_LIMIT):
    m, kdim = a.shape
    n = w.shape[2]
    tm, tn, tk = min(tm, m), min(tn, n), min(tk, kdim)
    nk = kdim // tk
    assert nk == 1 or (out_dtype == F32 and epilogue is None)
    in_specs = [pl.BlockSpec((tm, tk), lambda i, j, k: (i, k)),
                pl.BlockSpec((1, tk, tn), lambda i, j, k: (layer, k, j))]
    args = [a, w]
    if res is not None:
        in_specs.append(pl.BlockSpec((tm, tn), lambda i, j, k: (i, j)))
        args.append(res)
    return pl.pallas_call(
        functools.partial(_mm_body, nk=nk, epilogue=epilogue, has_res=res is not None),
        grid=(m // tm, n // tn, nk),
        in_specs=in_specs,
        out_specs=pl.BlockSpec((tm, tn), lambda i, j, k: (i, j)),
        out_shape=jax.ShapeDtypeStruct((m, n), out_dtype),
        compiler_params=_params("parallel", "parallel", "arbitrary", vmem_limit=vmem_limit),
    )(*args)


def _tables_body(pos_ref, inva_ref, invi_ref, ca_ref, sa_ref, ci_ref, si_ref):
    p = pos_ref[...]
    ang_a = p * inva_ref[...]
    ang_i = p * invi_ref[...]
    ca_ref[...] = jnp.cos(ang_a)
    sa_ref[...] = jnp.sin(ang_a)
    ci_ref[...] = jnp.cos(ang_i)
    si_ref[...] = jnp.sin(ang_i)


def _rope_tables(positions):
    b, s = positions.shape
    m = b * s
    pos = jnp.broadcast_to(positions.astype(F32).reshape(m, 1), (m, LANES))

    def inv_row(rot, period):
        inv = ROPE_THETA ** (-jnp.arange(0, rot, 2, dtype=F32) / rot)
        one = jnp.concatenate([inv, inv, jnp.zeros((period - rot,), F32)])
        return jnp.tile(one, LANES // period).reshape(1, LANES)

    tr = min(1024, m)
    spec = pl.BlockSpec((tr, LANES), lambda i: (i, 0))
    row = pl.BlockSpec((1, LANES), lambda i: (0, 0))
    sds = jax.ShapeDtypeStruct((m, LANES), F32)
    return pl.pallas_call(
        _tables_body,
        grid=(m // tr,),
        in_specs=[spec, row, row],
        out_specs=[spec] * 4,
        out_shape=[sds] * 4,
        compiler_params=_params("parallel"),
    )(pos, inv_row(A_ROPE_DIM, A_HEAD_DIM), inv_row(IDX_ROPE_DIM, IDX_DIM))


def _rope_group(x, cos, sin, first_half, half):
    rot = jnp.where(first_half, -pltpu.roll(x, LANES - half, 1), pltpu.roll(x, half, 1))
    return x * cos + sin * rot


def _first_half_mask(rows, period, half):
    lane = lax.broadcasted_iota(I32, (rows, LANES), 1)
    return (lane & (period - 1)) < half


def _rope_mm_body(a_ref, b_ref, cos_ref, sin_ref, o_ref, *, period, half, n_roped, scale_first):
    j = pl.program_id(1)
    acc = _dot(a_ref[...], b_ref[0])
    tm, tn = acc.shape
    first = _first_half_mask(tm, period, half)
    cos, sin = cos_ref[...], sin_ref[...]
    if n_roped is not None:
        roped = j < n_roped
        cos = jnp.where(roped, cos, 1.0)
        sin = jnp.where(roped, sin, 0.0)
    if scale_first:
        cos = cos * jnp.where(j == 0, Q_PRESCALE, 1.0)
        sin = sin * jnp.where(j == 0, Q_PRESCALE, 1.0)
    for g in range(tn // LANES):
        sl = slice(g * LANES, (g + 1) * LANES)
        o_ref[:, sl] = _rope_group(acc[:, sl], cos, sin, first, half).astype(o_ref.dtype)


def _rope_mm(xn, w, layer, col0, n_blocks, tn, cos, sin, out_dtype, *, period, half,
             n_roped=None, scale_first=False):
    m, d = xn.shape
    tm = min(1024, m)
    tab = pl.BlockSpec((tm, LANES), lambda i, j: (i, 0))
    return pl.pallas_call(
        functools.partial(_rope_mm_body, period=period, half=half, n_roped=n_roped,
                          scale_first=scale_first),
        grid=(m // tm, n_blocks),
        in_specs=[pl.BlockSpec((tm, d), lambda i, j: (i, 0)),
                  pl.BlockSpec((1, d, tn), lambda i, j: (layer, 0, col0 + j)),
                  tab, tab],
        out_specs=pl.BlockSpec((tm, tn), lambda i, j: (i, j)),
        out_shape=jax.ShapeDtypeStruct((m, n_blocks * tn), out_dtype),
        compiler_params=_params("parallel", "parallel"),
    )(xn, w, cos, sin)


def _sortable(score):
    bits = lax.bitcast_convert_type(score, I32)
    return jnp.where(bits < 0, bits ^ jnp.int32(0x7FFFFFFF), bits)


def _index_body(qi_ref, w_ref, ki_ref, bias_ref, kbf_scr, qm_scr, sc_scr, pk_scr, jthr_scr,
                *, tq, topk, scale, s_len):
    i = pl.program_id(1)
    n_chunks = s_len // tq
    n_live = i + 1

    @pl.when(i == 0)
    def _():
        kbf_scr[...] = ki_ref[...].astype(BF16)

    lane = lax.broadcasted_iota(I32, (tq, LANES), 1)
    for p in range(IDX_HEADS // 2):
        qp = qi_ref[:, p * LANES:(p + 1) * LANES]
        zero = jnp.zeros_like(qp)
        qm_scr[2 * p] = jnp.where(lane < IDX_DIM, qp, zero)
        qm_scr[2 * p + 1] = jnp.where(lane >= IDX_DIM, qp, zero)
    w_t = w_ref[...].T

    def chunk_rows(c):
        return pl.ds(pl.multiple_of(c * tq, tq), tq)

    def score_body(c, carry):
        rows = chunk_rows(c)
        kc = kbf_scr[rows, :]
        for h in range(0, IDX_HEADS, 2):
            rel = (jnp.maximum(_dot_nt(kc, qm_scr[h]), 0.0) * w_t[h:h + 1, :]
                   + jnp.maximum(_dot_nt(kc, qm_scr[h + 1]), 0.0) * w_t[h + 1:h + 2, :])
            if h == 0:
                sc_scr[rows, :] = rel
            else:
                sc_scr[rows, :] += rel
        return carry

    lax.fori_loop(0, n_live, score_body, 0)

    q_pos = i * tq + lax.broadcasted_iota(I32, (tq, tq), 1)
    k_off = lax.broadcasted_iota(I32, (tq, tq), 0)

    def key_body(c, carry):
        rows = chunk_rows(c)
        key = _sortable(sc_scr[rows, :] * scale)
        key = jnp.where(c * tq + k_off <= q_pos, key, jnp.int32(INT_MIN))
        sc_scr[rows, :] = lax.bitcast_convert_type(key, F32)
        return carry

    lax.fori_loop(0, n_live, key_body, 0)

    def load_keys(c):
        return lax.bitcast_convert_type(sc_scr[chunk_rows(c), :], I32)

    def count(pred):
        def hits(c):
            hit = pred(load_keys(c), c * tq + k_off).astype(I32)
            return hit.reshape(tq // 8, 8, tq).sum(axis=0)

        def pair_body(g, cnt8):
            return cnt8 + hits(2 * g) + hits(2 * g + 1)

        cnt8 = lax.fori_loop(0, jnp.right_shift(n_live, 1), pair_body, jnp.zeros((8, tq), I32))
        cnt8 = cnt8 + jnp.where((n_live & 1) == 1, hits(n_live - 1), 0)
        return cnt8.sum(axis=0, keepdims=True)

    full = q_pos[0:1, :] + 1 >= topk
    unknown = jnp.int32(s_len)

    def count16(pred):
        def hits(c):
            hit = jnp.where(pred(pk_scr[chunk_rows(c), :]), jnp.int16(1), jnp.int16(0))
            parts = hit.reshape(tq // 16, 16, tq)
            total = parts[0]
            for r in range(1, tq // 16):
                total = total + parts[r]
            return total

        def pair_body(g, cnt):
            return cnt + hits(2 * g) + hits(2 * g + 1)

        cnt = lax.fori_loop(0, jnp.right_shift(n_live, 1), pair_body, jnp.zeros((16, tq), I16))
        cnt = cnt + jnp.where((n_live & 1) == 1, hits(n_live - 1), jnp.int16(0))
        return cnt.astype(I32).sum(axis=0, keepdims=True)

    def bisect16(target, n_open):
        cnt0 = count16(lambda v: v >= 0)
        nonneg = cnt0 >= target
        prefix0 = jnp.where(nonneg, jnp.int32(0), jnp.int32(-32768))
        cntp0 = jnp.where(nonneg, cnt0, unknown)

        def cond(carry):
            it, _, _, still_open = carry
            return jnp.logical_and(it < 15, still_open > 0)

        def body(carry):
            it, prefix, cntp, _ = carry
            cand = prefix | jnp.left_shift(jnp.int32(1), jnp.int32(14) - it)
            cand16 = cand.astype(I16)
            cnt = count16(lambda v: v >= cand16)
            take = cnt >= target
            cntp = jnp.where(take, cnt, cntp)
            return it + 1, jnp.where(take, cand, prefix), cntp, n_open(cntp)

        _, prefix, cntp, _ = lax.while_loop(cond, body, (jnp.int32(0), prefix0, cntp0, n_open(cntp0)))
        return prefix, cntp

    def open_lanes(cnt_sel):
        return jnp.max(jnp.where(jnp.logical_and(full, cnt_sel != topk), 1, 0))

    def pack_hi_body(c, carry):
        pk_scr[chunk_rows(c), :] = jnp.right_shift(load_keys(c), 16).astype(I16)
        return carry

    lax.fori_loop(0, n_live, pack_hi_body, 0)
    thr_hi, _ = bisect16(topk, open_lanes)
    thr_hi16 = thr_hi.astype(I16)
    cnt_above = count16(lambda v: v > thr_hi16)

    def pack_lo_body(c, carry):
        key = load_keys(c)
        lo = (key & 0xFFFF) - 32768
        lo = jnp.where(jnp.right_shift(key, 16) == thr_hi, lo, -32768)
        pk_scr[chunk_rows(c), :] = lo.astype(I16)
        return carry

    lax.fori_loop(0, n_live, pack_lo_body, 0)
    thr_lo, cnt_lo = bisect16(topk - cnt_above, lambda cntp: open_lanes(cnt_above + cntp))
    thr = thr_hi * 65536 + (thr_lo + 32768)
    cnt_ge = cnt_above + cnt_lo

    tie = jnp.logical_and(cnt_ge > topk, thr != INT_MIN)
    jthr_scr[...] = jnp.full((8, tq), s_len, I32)

    @pl.when(jnp.max(tie.astype(I32)) > 0)
    def _():
        n_bits = max(1, (s_len - 1).bit_length())
        need = topk - count(lambda key, idx: key > thr)

        def idx_body(it, ans):
            cand = ans | jnp.left_shift(jnp.int32(1), jnp.int32(n_bits - 1) - it)
            cnt = count(lambda key, idx: jnp.logical_and(key == thr, idx < cand))
            return jnp.where(cnt < need, cand, ans)

        ans = lax.fori_loop(0, n_bits, idx_body, jnp.zeros((1, tq), I32))
        jthr_scr[...] = jnp.broadcast_to(jnp.where(tie, ans, s_len), (8, tq))

    jthr = jthr_scr[0:1, :]

    def out_body(c, carry):
        key = load_keys(c)
        idx = c * tq + k_off
        sel = jnp.logical_or(key > thr, jnp.logical_and(key == thr, idx <= jthr))
        sel = jnp.logical_and(sel, idx <= q_pos)
        bias_t = jnp.where(sel, 0.0, NEG)
        bias_ref[0, 0, c] = bias_t.T.astype(bias_ref.dtype)
        return carry

    lax.fori_loop(0, n_live, out_body, 0)

    def fill_body(c, carry):
        bias_ref[0, 0, c] = jnp.full((tq, tq), NEG, bias_ref.dtype)
        return carry

    lax.fori_loop(n_live, n_chunks, fill_body, 0)


def _index_mask(zi, kiwi, b, s_len, tq):
    nq = s_len // tq
    topk = min(TOPK_MAX, s_len // 4)
    return pl.pallas_call(
        functools.partial(_index_body, tq=tq, topk=topk, s_len=s_len,
                          scale=float((IDX_DIM * IDX_HEADS) ** -0.5)),
        grid=(b, nq),
        in_specs=[pl.BlockSpec((tq, IDX_WIDTH), lambda bi, i: (bi * nq + i, 0)),
                  pl.BlockSpec((tq, LANES), lambda bi, i: (bi * nq + i, 1)),
                  pl.BlockSpec((s_len, LANES), lambda bi, i: (bi, 0))],
        out_specs=pl.BlockSpec((1, 1, nq, tq, tq), lambda bi, i: (bi, i, 0, 0, 0)),
        out_shape=jax.ShapeDtypeStruct((b, nq, nq, tq, tq), BF16),
        scratch_shapes=[pltpu.VMEM((s_len, LANES), BF16),
                        pltpu.VMEM((IDX_HEADS, tq, LANES), BF16),
                        pltpu.VMEM((s_len, tq), F32),
                        pltpu.VMEM((s_len, tq), I16),
                        pltpu.VMEM((8, tq), I32)],
        compiler_params=_params("arbitrary", "arbitrary"),
    )(zi, kiwi, kiwi)


def _flash_body(q_ref, k_ref, v_ref, b_ref, o_ref, m_scr, l_scr, acc_scr, *, heads, nk, sub):
    i, j = pl.program_id(1), pl.program_id(2)
    tk = k_ref.shape[0]

    @pl.when(j == 0)
    def _():
        m_scr[...] = jnp.full(m_scr.shape, NEG, F32)
        l_scr[...] = jnp.zeros(l_scr.shape, F32)
        acc_scr[...] = jnp.zeros(acc_scr.shape, F32)

    @pl.when(j * sub <= i)
    def _():
        bias = jnp.concatenate([b_ref[0, 0, c] for c in range(sub)], axis=1).astype(F32)
        for h in range(heads):
            sl = slice(h * A_HEAD_DIM, (h + 1) * A_HEAD_DIM)
            s = _dot_nt(q_ref[:, sl], k_ref[:, sl]) + bias
            m_prev = m_scr[h]
            m_new = jnp.maximum(m_prev, jnp.max(s, axis=1, keepdims=True))
            alpha = jnp.exp2(m_prev - m_new)
            p = jnp.exp2(s - jnp.tile(m_new, (1, tk // LANES)))
            l_scr[h] = alpha * l_scr[h] + jnp.sum(p, axis=1, keepdims=True)
            acc_scr[:, sl] = alpha * acc_scr[:, sl] + _dot(p.astype(BF16), v_ref[:, sl])
            m_scr[h] = m_new

    @pl.when(j == nk - 1)
    def _():
        for h in range(heads):
            sl = slice(h * A_HEAD_DIM, (h + 1) * A_HEAD_DIM)
            o_ref[:, sl] = (acc_scr[:, sl] / l_scr[h]).astype(o_ref.dtype)


def _masked_attention(z, bias, b, s_len, bw, col, tq):
    nq = s_len // tq
    sub = min(FLASH_SUB, nq)
    nk = nq // sub
    tk = sub * tq
    heads = bw // A_HEAD_DIM
    kblk = lambda bi, i, j: bi * nk + jnp.minimum(j, i // sub)
    return pl.pallas_call(
        functools.partial(_flash_body, heads=heads, nk=nk, sub=sub),
        grid=(b, nq, nk),
        in_specs=[pl.BlockSpec((tq, bw), lambda bi, i, j: (bi * nq + i, col)),
                  pl.BlockSpec((tk, bw), lambda bi, i, j: (kblk(bi, i, j), col + 1)),
                  pl.BlockSpec((tk, bw), lambda bi, i, j: (kblk(bi, i, j), col + 2)),
                  pl.BlockSpec((1, 1, sub, tq, tq),
                               lambda bi, i, j: (bi, i, jnp.minimum(j, i // sub), 0, 0))],
        out_specs=pl.BlockSpec((tq, bw), lambda bi, i, j: (bi * nq + i, 0)),
        out_shape=jax.ShapeDtypeStruct((b * s_len, bw), BF16),
        scratch_shapes=[pltpu.VMEM((heads, tq, LANES), F32),
                        pltpu.VMEM((heads, tq, LANES), F32),
                        pltpu.VMEM((tq, bw), F32)],
        compiler_params=_params("parallel", "parallel", "arbitrary"),
    )(z, z, z, bias)


def _conformer_body(a_ref, g_ref, ah_ref, gh_ref, w_ref, bdw_ref, lng_ref, lnb_ref,
                    o_ref, hbuf, ybuf, *, width):
    i = pl.program_id(1)
    ts, bw = a_ref.shape
    ncg = bw // LANES
    h_main = a_ref[...].astype(F32) * jax.nn.sigmoid(g_ref[...].astype(F32))
    h_halo = ah_ref[...].astype(F32) * jax.nn.sigmoid(gh_ref[...].astype(F32))
    h_halo = jnp.where(i == 0, 0.0, h_halo)
    for c in range(ncg):
        sl = slice(c * LANES, (c + 1) * LANES)
        hbuf[c, 0:CONV_HALO, :] = h_halo[:, sl]
        hbuf[c, CONV_HALO:, :] = h_main[:, sl]

    def col_body(c, carry):
        wc = w_ref[c]
        acc = jnp.zeros((ts, LANES), F32)
        for k in range(width):
            off = CONV_HALO - (width - 1) + k
            acc = acc + hbuf[c, pl.ds(off, ts), :] * wc[k:k + 1, :]
        ybuf[c] = acc + bdw_ref[c]
        return carry

    lax.fori_loop(0, ncg, col_body, 0)

    tot = ybuf[0]
    for c in range(1, ncg):
        tot = tot + ybuf[c]
    mu = jnp.sum(tot, axis=1, keepdims=True) / bw
    sq = jnp.zeros((ts, LANES), F32)
    for c in range(ncg):
        d = ybuf[c] - mu
        sq = sq + d * d
    inv = lax.rsqrt(jnp.sum(sq, axis=1, keepdims=True) / bw + EPS)
    for c in range(ncg):
        y = (ybuf[c] - mu) * inv * lng_ref[c] + lnb_ref[c]
        o_ref[:, c * LANES:(c + 1) * LANES] = (y * jax.nn.sigmoid(y)).astype(o_ref.dtype)


def _lane_groups(v):
    return v.reshape(-1, 1, LANES)


def _conformer(z, b, s_len, bw, col, w_dw, b_dw, ln_g, ln_b, ts=256):
    width = w_dw.shape[0]
    ncg = bw // LANES
    ns = s_len // ts
    hb = ts // CONV_HALO
    w_cg = w_dw.reshape(width, ncg, LANES).transpose(1, 0, 2)
    main = lambda cb: pl.BlockSpec((ts, bw), lambda bi, i: (bi * ns + i, cb))
    halo = lambda cb: pl.BlockSpec(
        (CONV_HALO, bw), lambda bi, i: ((bi * ns + i) * hb - jnp.minimum(i, 1), cb))
    small = pl.BlockSpec((ncg, 1, LANES), lambda bi, i: (0, 0, 0))
    return pl.pallas_call(
        functools.partial(_conformer_body, width=width),
        grid=(b, ns),
        in_specs=[main(col), main(col + 1), halo(col), halo(col + 1),
                  pl.BlockSpec((ncg, width, LANES), lambda bi, i: (0, 0, 0)),
                  small, small, small],
        out_specs=pl.BlockSpec((ts, bw), lambda bi, i: (bi * ns + i, 0)),
        out_shape=jax.ShapeDtypeStruct((b * s_len, bw), BF16),
        scratch_shapes=[pltpu.VMEM((ncg, ts + CONV_HALO, LANES), F32),
                        pltpu.VMEM((ncg, ts, LANES), F32)],
        compiler_params=_params("parallel", "parallel"),
    )(z, z, z, z, w_cg, _lane_groups(b_dw), _lane_groups(ln_g), _lane_groups(ln_b))


def _gmlp_body(u_ref, v_ref, lng_ref, lnb_ref, w_ref, bias_ref, o_ref):
    ts, bw = u_ref.shape
    groups = bw // LANES
    u = jax.nn.gelu(u_ref[...].astype(F32))
    v = jax.nn.gelu(v_ref[...].astype(F32))
    mu = jnp.mean(v, axis=-1, keepdims=True)
    vc = v - mu
    var = jnp.mean(vc * vc, axis=-1, keepdims=True)
    vn = (vc * lax.rsqrt(var + EPS) * lng_ref[...] + lnb_ref[...]).astype(BF16)
    row = lax.broadcasted_iota(I32, (SG_CHUNK, SG_CHUNK), 0)
    col = lax.broadcasted_iota(I32, (SG_CHUNK, SG_CHUNK), 1)
    for g in range(groups):
        cs = slice(g * LANES, (g + 1) * LANES)
        wg = jnp.where(row >= col, w_ref[g], 0.0).astype(BF16)
        for c in range(ts // SG_CHUNK):
            rs = slice(c * SG_CHUNK, (c + 1) * SG_CHUNK)
            mixed = _dot(wg, vn[rs, cs]) + bias_ref[:, cs]
            o_ref[rs, cs] = (u[rs, cs] * mixed).astype(o_ref.dtype)


def _gmlp(z, m, bw, col, ln_g, ln_b, sg_w, sg_b, ts=256):
    groups = bw // LANES
    bias = jnp.broadcast_to(sg_b.T[:, :, None], (SG_CHUNK, groups, LANES)).reshape(SG_CHUNK, bw)
    row = pl.BlockSpec((1, bw), lambda i: (0, 0))
    return pl.pallas_call(
        _gmlp_body,
        grid=(m // ts,),
        in_specs=[pl.BlockSpec((ts, bw), lambda i: (i, col)),
                  pl.BlockSpec((ts, bw), lambda i: (i, col + 1)),
                  row, row,
                  pl.BlockSpec((groups, SG_CHUNK, SG_CHUNK), lambda i: (0, 0, 0)),
                  pl.BlockSpec((SG_CHUNK, bw), lambda i: (0, 0))],
        out_specs=pl.BlockSpec((ts, bw), lambda i: (i, 0)),
        out_shape=jax.ShapeDtypeStruct((m, bw), BF16),
        compiler_params=_params("parallel"),
    )(z, z, ln_g.reshape(1, bw), ln_b.reshape(1, bw), sg_w, bias)


def _shortconv_body(bg_ref, cg_ref, h_ref, cgh_ref, hh_ref, w_ref, o_ref, xbuf, *, width):
    i = pl.program_id(1)
    ts = bg_ref.shape[0]
    x_halo = cgh_ref[...].astype(F32) * hh_ref[...].astype(F32)
    xbuf[0:SC_HALO, :] = jnp.where(i == 0, 0.0, x_halo)
    xbuf[SC_HALO:, :] = cg_ref[...].astype(F32) * h_ref[...].astype(F32)
    acc = jnp.zeros(bg_ref.shape, F32)
    for k in range(width):
        off = SC_HALO - (width - 1) + k
        acc = acc + xbuf[pl.ds(off, ts), :] * w_ref[k:k + 1, :]
    o_ref[...] = (bg_ref[...].astype(F32) * acc).astype(o_ref.dtype)


def _shortconv(z, b, s_len, bw, col, w_conv, ts=256):
    width = w_conv.shape[0]
    ns = s_len // ts
    hb = ts // SC_HALO
    main = lambda cb: pl.BlockSpec((ts, bw), lambda bi, i: (bi * ns + i, cb))
    halo = lambda cb: pl.BlockSpec(
        (SC_HALO, bw), lambda bi, i: ((bi * ns + i) * hb - jnp.minimum(i, 1), cb))
    return pl.pallas_call(
        functools.partial(_shortconv_body, width=width),
        grid=(b, ns),
        in_specs=[main(col), main(col + 1), main(col + 2), halo(col + 1), halo(col + 2),
                  pl.BlockSpec((width, bw), lambda bi, i: (0, 0))],
        out_specs=pl.BlockSpec((ts, bw), lambda bi, i: (bi * ns + i, 0)),
        out_shape=jax.ShapeDtypeStruct((b * s_len, bw), BF16),
        scratch_shapes=[pltpu.VMEM((ts + SC_HALO, bw), F32)],
        compiler_params=_params("parallel", "parallel"),
    )(z, z, z, z, z, w_conv)


def _combine_body(xn_ref, wg_ref, bg_ref, *rest):
    branch_refs = rest[:N_BRANCH]
    wbr_ref, y_ref, acc_ref = rest[N_BRANCH:]
    n = pl.program_id(2)
    gate = jax.nn.sigmoid(_dot(xn_ref[...], wg_ref[0, 0]) + bg_ref[0, 0])
    o = branch_refs[0][...]
    for k in range(1, N_BRANCH):
        o = jnp.where(n == k, branch_refs[k][...], o)
    term = gate * _dot(o, wbr_ref[0, 0])

    @pl.when(n == 0)
    def _():
        acc_ref[...] = term

    @pl.when(jnp.logical_and(n > 0, n < N_BRANCH - 1))
    def _():
        acc_ref[...] += term

    @pl.when(n == N_BRANCH - 1)
    def _():
        y_ref[...] = (acc_ref[...] + term).astype(y_ref.dtype)


def _combine(xn, w_gate, b_gate, w_br, layer, branches, tm=1024, tn=512):
    m, d = xn.shape
    assert len(branches) == N_BRANCH
    bw = branches[0].shape[1]
    tm, tn = min(tm, m), min(tn, d)
    branch_spec = pl.BlockSpec((tm, bw), lambda i, j, n: (i, 0))
    return pl.pallas_call(
        _combine_body,
        grid=(m // tm, d // tn, N_BRANCH),
        in_specs=[pl.BlockSpec((tm, d), lambda i, j, n: (i, 0)),
                  pl.BlockSpec((1, 1, d, tn), lambda i, j, n: (layer, n, 0, j)),
                  pl.BlockSpec((1, 1, 1, tn), lambda i, j, n: (layer, n, 0, j))]
                 + [branch_spec] * N_BRANCH
                 + [pl.BlockSpec((1, 1, bw, tn), lambda i, j, n: (layer, n, 0, j))],
        out_specs=pl.BlockSpec((tm, tn), lambda i, j, n: (i, j)),
        out_shape=jax.ShapeDtypeStruct((m, d), BF16),
        scratch_shapes=[pltpu.VMEM((tm, tn), F32)],
        compiler_params=_params("parallel", "parallel", "arbitrary"),
    )(xn, w_gate, b_gate, *branches, w_br)


def _xattn_body(q_ref, k_ref, v_ref, o_ref, *, heads, scale):
    hd = q_ref.shape[1] // heads
    for h in range(heads):
        sl = slice(h * hd, (h + 1) * hd)
        s = _dot_nt(q_ref[:, sl], k_ref[:, sl]) * scale
        p = jnp.exp(s - jnp.max(s, axis=1, keepdims=True))
        l = jnp.sum(p, axis=1, keepdims=True)
        o_ref[:, sl] = (_dot(p.astype(BF16), v_ref[:, sl]) / l).astype(o_ref.dtype)


def _xattn(q, k, v, b, s_len, n_mem, tq=512):
    xw = q.shape[1]
    tq = min(tq, s_len)
    nq = s_len // tq
    kv = pl.BlockSpec((n_mem, xw), lambda bi, i: (bi, 0))
    return pl.pallas_call(
        functools.partial(_xattn_body, heads=X_HEADS, scale=float((xw // X_HEADS) ** -0.5)),
        grid=(b, nq),
        in_specs=[pl.BlockSpec((tq, xw), lambda bi, i: (bi * nq + i, 0)), kv, kv],
        out_specs=pl.BlockSpec((tq, xw), lambda bi, i: (bi * nq + i, 0)),
        out_shape=jax.ShapeDtypeStruct((b * s_len, xw), BF16),
        compiler_params=_params("parallel", "parallel"),
    )(q, k, v)


def kernel(x, mem, positions, norm_mix, w_in, conv_dw_w, conv_dw_b, conv_ln_g, conv_ln_b, sg_ln_g, sg_ln_b, sg_w, sg_b, sc_w, w_gate, b_gate, w_br, w_out, norm_xattn, norm_mem, xq_w, xk_w, xv_w, xo_w, norm_mlp, mlp_w1, mlp_w2, final_norm):
    b, s_len, d = x.shape
    n_mem = mem.shape[1]
    depth = w_in.shape[0]
    n_branch = w_gate.shape[1]
    bw = w_br.shape[1] // n_branch
    m = b * s_len
    assert n_branch == N_BRANCH and bw % A_HEAD_DIM == 0 and IDX_WIDTH % bw == 0
    tq = min(256, s_len)

    tables = _rope_tables(positions)
    cos_a, sin_a, cos_i, sin_i = tables
    h = x.reshape(m, d)
    mem2 = mem.reshape(b * n_mem, d)

    o_qi = 3 * bw
    o_ki = o_qi + IDX_WIDTH
    o_wi = o_ki + IDX_DIM
    o_b = o_wi + IDX_HEADS

    w_in_bf = w_in[:, :, :o_ki].astype(BF16)
    w_bcd = w_in[:, :, o_b:].astype(BF16)
    w_small = jnp.concatenate(
        [w_in[:, :, o_ki:o_wi], w_in[:, :, o_ki:o_wi], w_in[:, :, o_wi:o_b],
         jnp.zeros((depth, d, SMALL_N - 2 * IDX_DIM - IDX_HEADS), F32)], axis=2).astype(BF16)
    w_gate_bf = w_gate.astype(BF16)
    b_gate4 = b_gate.reshape(depth, n_branch, 1, d)
    w_br_bf = w_br.reshape(depth, n_branch, bw, d).astype(BF16)
    w_out_bf, xq_bf, xk_bf, xv_bf, xo_bf, w1_bf, w2_bf = (
        w.astype(BF16) for w in (w_out, xq_w, xk_w, xv_w, xo_w, mlp_w1, mlp_w2))

    for l in range(depth):
        xn = _rmsnorm(h, norm_mix[l], BF16)
        za = _rope_mm(xn, w_in_bf, l, 0, 3, bw, cos_a, sin_a, BF16, period=A_HEAD_DIM,
                      half=A_ROPE_DIM // 2, n_roped=2, scale_first=True)
        zi = _rope_mm(xn, w_in_bf, l, 3, IDX_WIDTH // bw, bw, cos_i, sin_i, BF16,
                      period=IDX_DIM, half=IDX_ROPE_DIM // 2)
        kiwi = _rope_mm(xn, w_small, l, 0, 2, LANES, cos_i, sin_i, F32,
                        period=IDX_DIM, half=IDX_ROPE_DIM // 2, n_roped=1)
        zm = _matmul(xn, w_bcd, l, BF16, tn=bw)

        bias = _index_mask(zi, kiwi, b, s_len, tq)
        mix_a = _masked_attention(za, bias, b, s_len, bw, 0, tq)
        mix_b = _conformer(zm, b, s_len, bw, 0, conv_dw_w[l], conv_dw_b[l], conv_ln_g[l], conv_ln_b[l])
        mix_c = _gmlp(zm, m, bw, 2, sg_ln_g[l], sg_ln_b[l], sg_w[l], sg_b[l])
        mix_d = _shortconv(zm, b, s_len, bw, 4, sc_w[l])

        y = _combine(xn, w_gate_bf, b_gate4, w_br_bf, l, (mix_a, mix_b, mix_c, mix_d))
        h = _matmul(y, w_out_bf, l, F32, res=h, tn=512)

        hn = _rmsnorm(h, norm_xattn[l], BF16)
        memn = _rmsnorm(mem2, norm_mem[l], BF16)
        qx = _matmul(hn, xq_bf, l, BF16)
        kx = _matmul(memn, xk_bf, l, BF16)
        vx = _matmul(memn, xv_bf, l, BF16)
        ox = _xattn(qx, kx, vx, b, s_len, n_mem)
        h = _matmul(ox, xo_bf, l, F32, res=h)

        hn = _rmsnorm(h, norm_mlp[l], BF16)
        a = _matmul(hn, w1_bf, l, BF16, epilogue="relu2")
        h = _matmul(a, w2_bf, l, F32, res=h, vmem_limit=VMEM_LIMIT_MAX)

    return _rmsnorm(h, final_norm, F32).reshape(b, s_len, d)
```
